```python
import jax, jax.numpy as jnp
from jax import lax
import numpy as np

D_MODEL = 1024
BATCH = 4
SEQ = 4096
DEPTH = 2
DEC_BATCH = 32
DEC_SEQ = 8
PAST_LEN = 16384
PAGE_SIZE = 128

W_A = D_MODEL // 4
CONV_A = 3
W_B = D_MODEL // 4
H_B = 4
DH_B = W_B // H_B
CONV_B = 4
RG_C = 8.0
H_C = 8
DH_C = 64
W_C = H_C * DH_C
Q_BLOCK = 128
SB_BIAS_INIT = -6.0
N_BRANCH = 3
SPLIT_SIZES = (W_A, W_A, W_A, W_B, W_C, W_C, W_C, D_MODEL, D_MODEL, D_MODEL)
IN_WIDTH = 3 * W_A + W_B + 3 * W_C + N_BRANCH * D_MODEL
D_FF = 2816
N_EXPERTS = 8
TOP_K = 2
D_FF_EXPERT = 3584
N_DENSE = (DEPTH + 1) // 2
N_MOE = DEPTH // 2
EPS = 1e-6

kernel_name = 'hybrid_conv_rglru_stickbreak_decoder_step'


def rmsnorm(x, g):
    xf = x.astype(jnp.float32)
    y = xf * lax.rsqrt(jnp.mean(xf * xf, axis=-1, keepdims=True) + EPS)
    return (y * g.astype(jnp.float32)).astype(x.dtype)


def causal_dwconv(u, hist, w):
    width, t = w.shape[0], u.shape[1]
    up = jnp.concatenate([hist.astype(u.dtype), u], axis=1)
    y = up[:, 0:t] * w[0]
    for j in range(1, width):
        y = y + up[:, j:j + t] * w[j]
    return y, up[:, t:]


def short_conv_mixer(b_g, c_g, x_a, hist, conv_w):
    y, new_hist = causal_dwconv(c_g * x_a, hist, conv_w)
    return b_g * y, new_hist


def _linear_combine(left, right):
    a_l, b_l = left
    a_r, b_r = right
    return a_l * a_r, a_r * b_l + b_r


def rglru_mixer(x_b, pos, hist, h0, conv_w, conv_bias, w_r, b_r, w_i, b_i, lam):
    xc, new_hist = causal_dwconv(x_b, hist, conv_w)
    xc = xc + conv_bias
    bsz, t, _ = xc.shape
    xh = xc.reshape(bsz, t, H_B, DH_B)
    r = jax.nn.sigmoid(jnp.einsum('bthd,hde->bthe', xh, w_r) + b_r).reshape(bsz, t, W_B)
    i = jax.nn.sigmoid(jnp.einsum('bthd,hde->bthe', xh, w_i) + b_i).reshape(bsz, t, W_B)
    log_a = (-RG_C * r.astype(jnp.float32)) * jax.nn.softplus(-lam.astype(jnp.float32))
    a = jnp.exp(log_a)
    mult = jnp.sqrt(-jnp.expm1(2.0 * log_a))
    mult = jnp.where((pos == 0)[None, :, None], 1.0, mult)
    b = mult * (i * xc).astype(jnp.float32)
    b = b.at[:, 0].add(a[:, 0] * h0.astype(jnp.float32))
    _, h = lax.associative_scan(_linear_combine, (a, b), axis=1)
    return h.astype(x_b.dtype), new_hist, h[:, -1].astype(h0.dtype)


def stick_breaking(q, k, v, q_pos, k_pos, bias):
    z = (jnp.einsum('bqhd,bkhd->bhqk', q.astype(jnp.float32), k.astype(jnp.float32)) * (DH_C ** -0.5)
         + bias.astype(jnp.float32)[None, :, None, None])
    mask = (k_pos[None, :] < q_pos[:, None])[None, None]
    log_keep = jnp.where(mask, jax.nn.log_sigmoid(-z), 0.0)
    after = lax.cumsum(log_keep, axis=3, reverse=True) - log_keep
    w = jnp.where(mask, jnp.exp(jax.nn.log_sigmoid(z) + after), 0.0)
    return jnp.einsum('bhqk,bkhd->bqhd', w, v.astype(jnp.float32))


def stick_breaking_prompt(q, k, v, bias):
    bsz, s = q.shape[:2]
    nb = s // Q_BLOCK
    qb = q.reshape(bsz, nb, Q_BLOCK, H_C, DH_C).transpose(1, 0, 2, 3, 4)
    pos = jnp.arange(s)
    q_pos = pos.reshape(nb, Q_BLOCK)
    o = lax.map(lambda args: stick_breaking(args[0], k, v, args[1], pos, bias), (qb, q_pos))
    return o.transpose(1, 0, 2, 3, 4).reshape(bsz, s, W_C)


def stick_breaking_sample(q, k_new, v_new, bias, cache_k, cache_v, page_table, layer):
    bsz, t = q.shape[:2]
    past = page_table.shape[1] * PAGE_SIZE
    k_past = cache_k[layer][page_table].reshape(bsz, past, H_C, DH_C)
    v_past = cache_v[layer][page_table].reshape(bsz, past, H_C, DH_C)
    k_all = jnp.concatenate([k_past, k_new.astype(k_past.dtype)], axis=1)
    v_all = jnp.concatenate([v_past, v_new.astype(v_past.dtype)], axis=1)
    o = stick_breaking(q, k_all, v_all, past + jnp.arange(t), jnp.arange(past + t), bias)
    return o.reshape(bsz, t, W_C)


def swiglu(h, w1, w3, w2):
    return (jax.nn.silu(h @ w1) * (h @ w3)) @ w2


def moe_swiglu(h, router, w1, w3, w2):
    logits = (h @ router).astype(jnp.float32)
    top_v, top_i = lax.top_k(logits, TOP_K)
    top_w = jax.nn.softmax(top_v, axis=-1)
    gates = jnp.sum(jax.nn.one_hot(top_i, N_EXPERTS, dtype=jnp.float32) * top_w[..., None], axis=-2)
    out = jnp.zeros(h.shape, jnp.float32)
    for e in range(N_EXPERTS):
        out = out + gates[..., e:e + 1] * swiglu(h, w1[e], w3[e], w2[e]).astype(jnp.float32)
    return out.astype(h.dtype)


def trunk_layer(x, pos, hist_a, hist_b, h0, attend, ffn,
                norm_mix, w_in, conv_a_w, conv_b_w, conv_b_bias, rg_w_r, rg_b_r, rg_w_i, rg_b_i,
                rg_lambda, sb_bias, w_br_a, w_br_b, w_br_c, w_out, norm_ffn):
    bsz, t, _ = x.shape
    h = rmsnorm(x, norm_mix)
    cuts = np.cumsum(SPLIT_SIZES)[:-1].tolist()
    b_g, c_g, x_a, x_b, q, k, v, g_a, g_b, g_c = jnp.split(h @ w_in, cuts, axis=-1)
    y_a, new_hist_a = short_conv_mixer(b_g, c_g, x_a, hist_a, conv_a_w)
    y_b, new_hist_b, h_last = rglru_mixer(x_b, pos, hist_b, h0, conv_b_w, conv_b_bias,
                                          rg_w_r, rg_b_r, rg_w_i, rg_b_i, rg_lambda)
    q = q.reshape(bsz, t, H_C, DH_C)
    k = k.reshape(bsz, t, H_C, DH_C)
    v = v.reshape(bsz, t, H_C, DH_C)
    y_c = attend(q, k, v, sb_bias).astype(x.dtype)
    merged = (jax.nn.sigmoid(g_a) * (y_a @ w_br_a)
              + jax.nn.sigmoid(g_b) * (y_b @ w_br_b)
              + jax.nn.sigmoid(g_c) * (y_c @ w_br_c))
    x = x + merged @ w_out
    x = x + ffn(rmsnorm(x, norm_ffn))
    return x, (k, v, new_hist_a, new_hist_b, h_last)


def setup_inputs(seed: int = 0) -> dict:
    key = jax.random.key(seed)
    keys = jax.random.split(key, 40)
    counter = iter(range(40))
    f32 = jnp.float32

    def nrm(shape, scale):
        return jax.random.normal(keys[next(counter)], shape, f32) * scale

    n_pages = PAST_LEN // PAGE_SIZE
    n_used = DEC_BATCH * n_pages
    n_phys = n_used + n_used // 4
    x_prompt = nrm((BATCH, SEQ, D_MODEL), 1.0)
    x_sample = nrm((DEC_BATCH, DEC_SEQ, D_MODEL), 1.0)
    cache_k = nrm((DEPTH, n_phys, PAGE_SIZE, H_C, DH_C), 1.0)
    cache_v = nrm((DEPTH, n_phys, PAGE_SIZE, H_C, DH_C), 1.0)
    state_conv_a = nrm((DEPTH, DEC_BATCH, CONV_A - 1, W_A), 1.0)
    state_conv_b = nrm((DEPTH, DEC_BATCH, CONV_B - 1, W_B), 1.0)
    state_h = nrm((DEPTH, DEC_BATCH, W_B), 1.0)
    page_table = jax.random.permutation(keys[next(counter)], n_phys)[:n_used].reshape(
        DEC_BATCH, n_pages).astype(jnp.int32)
    norm_mix = 1.0 + nrm((DEPTH, D_MODEL), 0.1)
    w_in = nrm((DEPTH, D_MODEL, IN_WIDTH), D_MODEL ** -0.5)
    conv_a_w = nrm((DEPTH, CONV_A, W_A), CONV_A ** -0.5)
    conv_b_w = nrm((DEPTH, CONV_B, W_B), CONV_B ** -0.5)
    conv_b_bias = nrm((DEPTH, W_B), 0.02)
    rg_w_r = nrm((DEPTH, H_B, DH_B, DH_B), DH_B ** -0.5)
    rg_b_r = nrm((DEPTH, H_B, DH_B), 0.02)
    rg_w_i = nrm((DEPTH, H_B, DH_B, DH_B), DH_B ** -0.5)
    rg_b_i = nrm((DEPTH, H_B, DH_B), 0.02)
    a_c = jax.random.uniform(keys[next(counter)], (DEPTH, W_B), f32, 0.9, 0.999)
    s = a_c ** (1.0 / RG_C)
    rg_lambda = jnp.log(s) - jnp.log1p(-s)
    sb_bias = SB_BIAS_INIT + nrm((DEPTH, H_C), 0.5)
    w_br_a = nrm((DEPTH, W_A, D_MODEL), W_A ** -0.5)
    w_br_b = nrm((DEPTH, W_B, D_MODEL), W_B ** -0.5)
    w_br_c = nrm((DEPTH, W_C, D_MODEL), W_C ** -0.5)
    w_out = nrm((DEPTH, D_MODEL, D_MODEL), D_MODEL ** -0.5)
    norm_ffn = 1.0 + nrm((DEPTH, D_MODEL), 0.1)
    ffn_w1 = nrm((N_DENSE, D_MODEL, D_FF), D_MODEL ** -0.5)
    ffn_w3 = nrm((N_DENSE, D_MODEL, D_FF), D_MODEL ** -0.5)
    ffn_w2 = nrm((N_DENSE, D_FF, D_MODEL), D_FF ** -0.5)
    moe_router = nrm((N_MOE, D_MODEL, N_EXPERTS), D_MODEL ** -0.5)
    moe_w1 = nrm((N_MOE, N_EXPERTS, D_MODEL, D_FF_EXPERT), D_MODEL ** -0.5)
    moe_w3 = nrm((N_MOE, N_EXPERTS, D_MODEL, D_FF_EXPERT), D_MODEL ** -0.5)
    moe_w2 = nrm((N_MOE, N_EXPERTS, D_FF_EXPERT, D_MODEL), D_FF_EXPERT ** -0.5)
    norm_final = 1.0 + nrm((D_MODEL,), 0.1)
    return {'x_prompt': x_prompt, 'x_sample': x_sample, 'cache_k': cache_k, 'cache_v': cache_v,
            'state_conv_a': state_conv_a, 'state_conv_b': state_conv_b, 'state_h': state_h,
            'page_table': page_table, 'norm_mix': norm_mix, 'w_in': w_in, 'conv_a_w': conv_a_w,
            'conv_b_w': conv_b_w, 'conv_b_bias': conv_b_bias, 'rg_w_r': rg_w_r, 'rg_b_r': rg_b_r,
            'rg_w_i': rg_w_i, 'rg_b_i': rg_b_i, 'rg_lambda': rg_lambda, 'sb_bias': sb_bias,
            'w_br_a': w_br_a, 'w_br_b': w_br_b, 'w_br_c': w_br_c, 'w_out': w_out, 'norm_ffn': norm_ffn,
            'ffn_w1': ffn_w1, 'ffn_w3': ffn_w3, 'ffn_w2': ffn_w2, 'moe_router': moe_router,
            'moe_w1': moe_w1, 'moe_w3': moe_w3, 'moe_w2': moe_w2, 'norm_final': norm_final}


def reference(x_prompt, x_sample, cache_k, cache_v, state_conv_a, state_conv_b, state_h, page_table,
              norm_mix, w_in, conv_a_w, conv_b_w, conv_b_bias, rg_w_r, rg_b_r, rg_w_i, rg_b_i,
              rg_lambda, sb_bias, w_br_a, w_br_b, w_br_c, w_out, norm_ffn, ffn_w1, ffn_w3, ffn_w2,
              moe_router, moe_w1, moe_w3, moe_w2, norm_final):
    bp, sp = x_prompt.shape[:2]
    ts = x_sample.shape[1]
    pos_p = jnp.arange(sp)
    pos_s = page_table.shape[1] * PAGE_SIZE + jnp.arange(ts)
    zeros_a = jnp.zeros((bp, CONV_A - 1, W_A), x_prompt.dtype)
    zeros_b = jnp.zeros((bp, CONV_B - 1, W_B), x_prompt.dtype)
    zeros_h = jnp.zeros((bp, W_B), x_prompt.dtype)
    xp, xs = x_prompt, x_sample
    st_p, st_s = [], []
    for l in range(DEPTH):
        i = l // 2
        if l % 2 == 0:
            ffn = lambda h, i=i: swiglu(h, ffn_w1[i], ffn_w3[i], ffn_w2[i])
        else:
            ffn = lambda h, i=i: moe_swiglu(h, moe_router[i], moe_w1[i], moe_w3[i], moe_w2[i])
        lw = (norm_mix[l], w_in[l], conv_a_w[l], conv_b_w[l], conv_b_bias[l], rg_w_r[l], rg_b_r[l],
              rg_w_i[l], rg_b_i[l], rg_lambda[l], sb_bias[l], w_br_a[l], w_br_b[l], w_br_c[l], w_out[l],
              norm_ffn[l])
        xp, sp_l = trunk_layer(xp, pos_p, zeros_a, zeros_b, zeros_h, stick_breaking_prompt, ffn, *lw)
        attend_s = lambda q, k, v, b, l=l: stick_breaking_sample(q, k, v, b, cache_k, cache_v, page_table, l)
        xs, ss_l = trunk_layer(xs, pos_s, state_conv_a[l], state_conv_b[l], state_h[l], attend_s, ffn, *lw)
        st_p.append(sp_l)
        st_s.append(ss_l)
    y_prompt = rmsnorm(xp, norm_final)
    y_sample = rmsnorm(xs, norm_final)
    k_prompt = jnp.stack([s[0] for s in st_p])
    v_prompt = jnp.stack([s[1] for s in st_p])
    conv_a_prompt = jnp.stack([s[2] for s in st_p])
    conv_b_prompt = jnp.stack([s[3] for s in st_p])
    h_prompt = jnp.stack([s[4] for s in st_p])
    k_sample = jnp.stack([s[0] for s in st_s])
    v_sample = jnp.stack([s[1] for s in st_s])
    conv_a_sample = jnp.stack([s[2] for s in st_s])
    conv_b_sample = jnp.stack([s[3] for s in st_s])
    h_sample = jnp.stack([s[4] for s in st_s])
    return (y_prompt, y_sample, k_prompt, v_prompt, conv_a_prompt, conv_b_prompt, h_prompt,
            k_sample, v_sample, conv_a_sample, conv_b_sample, h_sample)
```

```python
import functools

import jax
import jax.numpy as jnp
from jax import lax
from jax.experimental import pallas as pl
from jax.experimental.pallas import tpu as pltpu

F32 = jnp.float32
BF16 = jnp.bfloat16

D_MODEL = 1024
BATCH = 4
SEQ = 4096
DEC_BATCH = 32
DEC_SEQ = 8
PAGE_SIZE = 128
W_A = 256
W_B = 256
H_B = 4
DH_B = 64
CONV_A = 3
CONV_B = 4
RG_C = 8.0
H_C = 8
DH_C = 64
W_C = 512
IN_WIDTH = 5632
N_EXPERTS = 8
EPS = 1e-6

N_PROMPT = BATCH * SEQ
N_SAMPLE = DEC_BATCH * DEC_SEQ
N_TOK = N_PROMPT + N_SAMPLE

COL_Q = 1024
COL_K = 1536
COL_V = 2048
COL_G = 2560

SUBLANES = 8
LANES = 128
VMEM_LIMIT = 56 * 1024 * 1024

ATT_TQ = 256
PAGES_PER_STEP = 16


def _cparams(sem):
    return pltpu.CompilerParams(dimension_semantics=sem, vmem_limit_bytes=VMEM_LIMIT)


def _rmsnorm(x, g):
    y = x * lax.rsqrt(jnp.mean(x * x, axis=-1, keepdims=True) + EPS)
    return y * g


def _inproj_kernel(x_ref, g_ref, w_ref, o_ref, h_scr):
    @pl.when(pl.program_id(1) == 0)
    def _():
        h_scr[...] = _rmsnorm(x_ref[...], g_ref[...]).astype(BF16)

    o_ref[...] = jnp.dot(h_scr[...], w_ref[...], preferred_element_type=F32)


def _in_proj(x, g, w_bf16):
    tm, tn = 1280, 512
    return pl.pallas_call(
        _inproj_kernel,
        grid=(N_TOK // tm, IN_WIDTH // tn),
        in_specs=[
            pl.BlockSpec((tm, D_MODEL), lambda i, j: (i, 0)),
            pl.BlockSpec((1, D_MODEL), lambda i, j: (0, 0)),
            pl.BlockSpec((D_MODEL, tn), lambda i, j: (0, j)),
        ],
        out_specs=pl.BlockSpec((tm, tn), lambda i, j: (i, j)),
        out_shape=jax.ShapeDtypeStruct((N_TOK, IN_WIDTH), F32),
        scratch_shapes=[pltpu.VMEM((tm, D_MODEL), BF16)],
        compiler_params=_cparams(("parallel", "arbitrary")),
        name="in_proj",
    )(x, g.reshape(1, D_MODEL), w_bf16)


def _softplus(x):
    return jnp.maximum(x, 0.0) + jnp.log1p(jnp.exp(-jnp.abs(x)))


def _mixers_kernel(p_ref, ha_ref, hb_ref, h0_ref, wa_ref, wb_ref, bb_ref, wr_ref, br_ref,
                   wi_ref, bi_ref, lam_ref, *rest, tt, first_pos_zero, aliased):
    if aliased:
        rest = rest[1:]
    y_ref, sa_ref, sb_ref, hl_ref, ua_buf, xb_buf, hc_scr = rest
    ti = pl.program_id(1)

    @pl.when(ti == 0)
    def _():
        ua_buf[0:SUBLANES, :] = ha_ref[...]
        xb_buf[0:SUBLANES, :] = hb_ref[...]
        hc_scr[...] = h0_ref[...]

    bg = p_ref[:, 0:W_A]
    cg = p_ref[:, W_A:2 * W_A]
    xa = p_ref[:, 2 * W_A:3 * W_A]
    xb = p_ref[:, 3 * W_A:3 * W_A + W_B]

    ua = cg * xa
    ua_buf[SUBLANES:SUBLANES + tt, :] = ua
    ya = ua_buf[pl.ds(SUBLANES - 2, tt), :] * wa_ref[0:1, :]
    ya = ya + ua_buf[pl.ds(SUBLANES - 1, tt), :] * wa_ref[1:2, :]
    ya = ya + ua * wa_ref[2:3, :]
    y_ref[:, 0:W_A] = bg * ya

    xb_buf[SUBLANES:SUBLANES + tt, :] = xb
    xc = xb_buf[pl.ds(SUBLANES - 3, tt), :] * wb_ref[0:1, :]
    xc = xc + xb_buf[pl.ds(SUBLANES - 2, tt), :] * wb_ref[1:2, :]
    xc = xc + xb_buf[pl.ds(SUBLANES - 1, tt), :] * wb_ref[2:3, :]
    xc = xc + xb * wb_ref[3:4, :]
    xc = xc + bb_ref[...]
    xcb = xc.astype(BF16)
    r = jax.nn.sigmoid(jnp.dot(xcb, wr_ref[...], preferred_element_type=F32) + br_ref[...])
    ig = jax.nn.sigmoid(jnp.dot(xcb, wi_ref[...], preferred_element_type=F32) + bi_ref[...])
    log_a = (-RG_C * r) * _softplus(-lam_ref[...])
    a = jnp.exp(log_a)
    mult = jnp.sqrt(1.0 - jnp.exp(2.0 * log_a))
    row = lax.broadcasted_iota(jnp.int32, (tt, W_B), 0)
    if first_pos_zero:
        mult = jnp.where((row == 0) & (ti == 0), 1.0, mult)
    b = mult * (ig * xc)

    s = 1
    while s < tt:
        keep = row >= s
        a_sh = pltpu.roll(a, s, 0)
        b_sh = pltpu.roll(b, s, 0)
        b = jnp.where(keep, a * b_sh + b, b)
        a = jnp.where(keep, a * a_sh, a)
        s *= 2
    h = a * hc_scr[0:1, :] + b
    y_ref[:, W_A:W_A + W_B] = h

    last8_a = ua_buf[tt:tt + SUBLANES, :]
    last8_b = xb_buf[tt:tt + SUBLANES, :]
    hlast = jnp.broadcast_to(h[tt - 1:tt, :], (SUBLANES, W_B))
    ua_buf[0:SUBLANES, :] = last8_a
    xb_buf[0:SUBLANES, :] = last8_b
    hc_scr[...] = hlast
    sa_ref[...] = last8_a
    sb_ref[...] = last8_b
    hl_ref[...] = hlast


def _mixers(proj, hist_a8, hist_b8, h0_8, lw, *, nseq, t, row0, first_pos_zero, y_prev=None):
    tt = min(t, 512)
    nt = t // tt
    rb0 = row0 // tt
    kern = functools.partial(_mixers_kernel, tt=tt, first_pos_zero=first_pos_zero,
                             aliased=y_prev is not None)
    seq3 = pl.BlockSpec((None, SUBLANES, W_A), lambda s, i: (s, 0, 0))
    full = lambda shape: pl.BlockSpec(shape, lambda s, i: (0,) * len(shape))
    in_specs = [
        pl.BlockSpec((tt, 4 * W_A), lambda s, i: (rb0 + s * nt + i, 0)),
        seq3, seq3, seq3,
        full((CONV_A, W_A)), full((CONV_B, W_B)), full((1, W_B)),
        full((W_B, W_B)), full((1, W_B)), full((W_B, W_B)), full((1, W_B)), full((1, W_B)),
    ]
    args = [proj, hist_a8, hist_b8, h0_8, lw["conv_a_w"], lw["conv_b_w"], lw["conv_b_bias"],
            lw["rg_w_r"], lw["rg_b_r"], lw["rg_w_i"], lw["rg_b_i"], lw["rg_lambda"]]
    aliases = {}
    if y_prev is not None:
        in_specs.append(pl.BlockSpec(memory_space=pl.ANY))
        args.append(y_prev)
        aliases = {len(args) - 1: 0}
    st = jax.ShapeDtypeStruct((nseq, SUBLANES, W_A), F32)
    return pl.pallas_call(
        kern,
        grid=(nseq, nt),
        in_specs=in_specs,
        out_specs=[
            pl.BlockSpec((tt, W_A + W_B), lambda s, i: (rb0 + s * nt + i, 0)),
            seq3, seq3, seq3,
        ],
        out_shape=[jax.ShapeDtypeStruct((N_TOK, W_A + W_B), F32), st, st, st],
        scratch_shapes=[
            pltpu.VMEM((SUBLANES + tt, W_A), F32),
            pltpu.VMEM((SUBLANES + tt, W_B), F32),
            pltpu.VMEM((SUBLANES, W_B), F32),
        ],
        input_output_aliases=aliases,
        compiler_params=_cparams(("parallel", "arbitrary")),
        name="mixers_t%d" % t,
    )(*args)


def _sb_chunk(qm, bias, kc, vc, u, carry, mask=None, transposed_kv=False):
    nt = (((1,), (1,)), ((), ()))
    if transposed_kv:
        z = jnp.dot(qm, kc, preferred_element_type=F32) + bias
    else:
        z = lax.dot_general(qm, kc, nt, preferred_element_type=F32) + bias
    ls = jnp.minimum(z, 0.0) - jnp.log1p(jnp.exp(-jnp.abs(z)))
    lk = ls - z
    if mask is not None:
        lk = jnp.where(mask, lk, 0.0)
    lk_hi = lk.astype(BF16)
    lk_lo = (lk - lk_hi.astype(F32)).astype(BF16)
    after = (jnp.dot(lk_hi, u, preferred_element_type=F32)
             + jnp.dot(lk_lo, u, preferred_element_type=F32))
    total = after[:, 0:1] + lk[:, 0:1]
    w = jnp.exp(ls + after + carry)
    if mask is not None:
        w = jnp.where(mask, w, 0.0)
    w = w.astype(BF16)
    if transposed_kv:
        pv = lax.dot_general(w, vc, nt, preferred_element_type=F32)
    else:
        pv = jnp.dot(w, vc, preferred_element_type=F32)
    return pv, total


def _attn_prompt_kernel(bias_ref, q_ref, k_ref, v_ref, u_ref, o_ref, kb_scr, vb_scr, acc_scr, cr_scr):
    hp = pl.program_id(1)
    qi = pl.program_id(2)
    tq = ATT_TQ

    @pl.when(qi == 0)
    def _():
        kb_scr[...] = k_ref[...].astype(BF16)
        vb_scr[...] = v_ref[...].astype(BF16)

    q = q_ref[...] * (DH_C ** -0.5)
    lane = lax.broadcasted_iota(jnp.int32, (tq, LANES), 1)
    row = lax.broadcasted_iota(jnp.int32, (tq, tq), 0)
    col = lax.broadcasted_iota(jnp.int32, (tq, tq), 1)
    causal = col < row
    u = u_ref[...]
    for hh in range(2):
        in_head = (lane < DH_C) if hh == 0 else (lane >= DH_C)
        qm = jnp.where(in_head, q, 0.0).astype(BF16)
        bias = bias_ref[2 * hp + hh]
        start = pl.multiple_of(qi * tq, tq)
        pv, total = _sb_chunk(qm, bias, kb_scr[pl.ds(start, tq), :], vb_scr[pl.ds(start, tq), :],
                              u, 0.0, mask=causal)
        acc_scr[hh] = pv
        cr_scr[...] = jnp.broadcast_to(total, (tq, LANES))

        def body(jj, _):
            st = pl.multiple_of((qi - jj) * tq, tq)
            pv, total = _sb_chunk(qm, bias, kb_scr[pl.ds(st, tq), :], vb_scr[pl.ds(st, tq), :],
                                  u, cr_scr[:, 0:1])
            acc_scr[hh] += pv
            cr_scr[...] += jnp.broadcast_to(total, (tq, LANES))
            return 0

        lax.fori_loop(1, qi + 1, body, 0)
    o_ref[...] = jnp.where(lane < DH_C, acc_scr[0], acc_scr[1])


def _attn_prompt(proj, sb_bias, u):
    tq = ATT_TQ
    nq = SEQ // tq
    grid_spec = pltpu.PrefetchScalarGridSpec(
        num_scalar_prefetch=1,
        grid=(BATCH, H_C // 2, nq),
        in_specs=[
            pl.BlockSpec((tq, LANES), lambda b, hp, qi, s: (b * nq + qi, COL_Q // LANES + hp)),
            pl.BlockSpec((SEQ, LANES), lambda b, hp, qi, s: (b, COL_K // LANES + hp)),
            pl.BlockSpec((SEQ, LANES), lambda b, hp, qi, s: (b, COL_V // LANES + hp)),
            pl.BlockSpec((tq, tq), lambda b, hp, qi, s: (0, 0)),
        ],
        out_specs=pl.BlockSpec((tq, LANES), lambda b, hp, qi, s: (b * nq + qi, hp)),
        scratch_shapes=[
            pltpu.VMEM((SEQ, LANES), BF16),
            pltpu.VMEM((SEQ, LANES), BF16),
            pltpu.VMEM((2, tq, LANES), F32),
            pltpu.VMEM((tq, LANES), F32),
        ],
    )
    return pl.pallas_call(
        _attn_prompt_kernel,
        grid_spec=grid_spec,
        out_shape=jax.ShapeDtypeStruct((N_TOK, W_C), F32),
        compiler_params=_cparams(("parallel", "parallel", "arbitrary")),
        name="attn_prompt",
    )(sb_bias, proj, proj, proj, u)


def _attn_sample_kernel(pt_ref, bias_ref, q_ref, kn_ref, vn_ref, u_ref, *rest):
    npg = PAGES_PER_STEP
    k_refs = rest[:npg]
    v_refs = rest[npg:2 * npg]
    o_ref, qbd_scr, bias_scr, kn_scr, vn_scr, acc_scr, cr_scr = rest[2 * npg + 1:]
    j = pl.program_id(1)
    nrow = H_C * DEC_SEQ
    u = u_ref[0:PAGE_SIZE, 0:PAGE_SIZE]

    @pl.when(j == 0)
    def _():
        row = lax.broadcasted_iota(jnp.int32, (nrow, W_C), 0)
        lane = lax.broadcasted_iota(jnp.int32, (nrow, W_C), 1)
        q8 = q_ref[...] * (DH_C ** -0.5)
        qt = jnp.concatenate([q8] * H_C, axis=0)
        qbd_scr[...] = jnp.where(row // DEC_SEQ == lane // DH_C, qt, 0.0).astype(BF16)
        rowb = lax.broadcasted_iota(jnp.int32, (nrow, LANES), 0)
        bias = jnp.zeros((nrow, LANES), F32)
        for h in range(H_C):
            bias = jnp.where(rowb // DEC_SEQ == h, bias_ref[h], bias)
        bias_scr[...] = bias
        kn_scr[...] = jnp.zeros((PAGE_SIZE, W_C), BF16)
        vn_scr[...] = jnp.zeros((PAGE_SIZE, W_C), BF16)
        kn_scr[0:DEC_SEQ, :] = kn_ref[...].astype(BF16)
        vn_scr[0:DEC_SEQ, :] = vn_ref[...].astype(BF16)
        rowc = lax.broadcasted_iota(jnp.int32, (nrow, PAGE_SIZE), 0)
        colc = lax.broadcasted_iota(jnp.int32, (nrow, PAGE_SIZE), 1)
        pv, total = _sb_chunk(qbd_scr[...], bias_scr[:, 0:1], kn_scr[...], vn_scr[...], u, 0.0,
                              mask=colc < rowc % DEC_SEQ)
        acc_scr[...] = pv
        cr_scr[...] = jnp.broadcast_to(total, (nrow, LANES))

    for p in range(npg):
        pv, total = _sb_chunk(qbd_scr[...], bias_scr[:, 0:1], k_refs[p][...].astype(BF16),
                              v_refs[p][...].astype(BF16), u, cr_scr[:, 0:1], transposed_kv=True)
        acc_scr[...] += pv
        cr_scr[...] += jnp.broadcast_to(total, (nrow, LANES))

    @pl.when(j == pl.num_programs(1) - 1)
    def _():
        lane = lax.broadcasted_iota(jnp.int32, (DEC_SEQ, W_C), 1)
        out = jnp.zeros((DEC_SEQ, W_C), F32)
        for h in range(H_C):
            out = out + jnp.where(lane // DH_C == h, acc_scr[h * DEC_SEQ:(h + 1) * DEC_SEQ, :], 0.0)
        o_ref[...] = out


def _attn_sample(proj, y_c, cache_k4, cache_v4, page_table, layer, sb_bias, u):
    npg = PAGES_PER_STEP
    n_pages = page_table.shape[1]
    nsteps = n_pages // npg
    rb0 = N_PROMPT // DEC_SEQ

    def row_spec(colblk):
        return pl.BlockSpec((DEC_SEQ, W_C), lambda b, j, pt, s: (rb0 + b, colblk))

    def page_spec(p):
        return pl.BlockSpec((None, None, W_C, PAGE_SIZE),
                            lambda b, j, pt, s: (layer, pt[b, n_pages - 1 - (j * npg + p)], 0, 0))

    grid_spec = pltpu.PrefetchScalarGridSpec(
        num_scalar_prefetch=2,
        grid=(DEC_BATCH, nsteps),
        in_specs=([row_spec(COL_Q // W_C), row_spec(COL_K // W_C), row_spec(COL_V // W_C),
                   pl.BlockSpec((ATT_TQ, ATT_TQ), lambda b, j, pt, s: (0, 0))]
                  + [page_spec(p) for p in range(npg)]
                  + [page_spec(p) for p in range(npg)]
                  + [pl.BlockSpec(memory_space=pl.ANY)]),
        out_specs=pl.BlockSpec((DEC_SEQ, W_C), lambda b, j, pt, s: (rb0 + b, 0)),
        scratch_shapes=[
            pltpu.VMEM((H_C * DEC_SEQ, W_C), BF16),
            pltpu.VMEM((H_C * DEC_SEQ, LANES), F32),
            pltpu.VMEM((PAGE_SIZE, W_C), BF16),
            pltpu.VMEM((PAGE_SIZE, W_C), BF16),
            pltpu.VMEM((H_C * DEC_SEQ, W_C), F32),
            pltpu.VMEM((H_C * DEC_SEQ, LANES), F32),
        ],
    )
    n_in = 2 + 4 + 2 * npg
    return pl.pallas_call(
        _attn_sample_kernel,
        grid_spec=grid_spec,
        out_shape=jax.ShapeDtypeStruct((N_TOK, W_C), F32),
        input_output_aliases={n_in: 0},
        compiler_params=_cparams(("parallel", "arbitrary")),
        name="attn_sample",
    )(page_table, sb_bias, proj, proj, proj, u, *([cache_k4] * npg), *([cache_v4] * npg), y_c)


def _merge(y_ab, y_c, proj, x, lw):
    tm = 640
    rows = lambda w, c: pl.BlockSpec((tm, w), lambda i: (i, c))
    full = lambda shape: pl.BlockSpec(shape, lambda i: (0, 0))
    return pl.pallas_call(
        _merge_kernel_split,
        grid=(N_TOK // tm,),
        in_specs=[
            rows(W_A + W_B, 0), rows(W_C, 0),
            *[pl.BlockSpec((tm, 512), functools.partial(lambda i, c: (i, c), c=COL_G // 512 + k))
              for k in range(6)],
            rows(D_MODEL, 0),
            full((W_A, D_MODEL)), full((W_B, D_MODEL)), full((W_C, D_MODEL)),
            full((D_MODEL, D_MODEL)), full((1, D_MODEL)),
        ],
        out_specs=[rows(D_MODEL, 0), rows(D_MODEL, 0)],
        out_shape=[jax.ShapeDtypeStruct((N_TOK, D_MODEL), F32),
                   jax.ShapeDtypeStruct((N_TOK, D_MODEL), BF16)],
        compiler_params=_cparams(("parallel",)),
        name="merge",
    )(y_ab, y_c, proj, proj, proj, proj, proj, proj, x, lw["w_br_a"], lw["w_br_b"], lw["w_br_c"],
      lw["w_out"], lw["norm_ffn"])


def _merge_kernel_split(yab_ref, yc_ref, ga0, ga1, gb0, gb1, gc0, gc1, x_ref, wa_ref, wb_ref,
                        wc_ref, wo_ref, gn_ref, xo_ref, h2_ref):
    ya = yab_ref[:, 0:W_A].astype(BF16)
    yb = yab_ref[:, W_A:W_A + W_B].astype(BF16)
    yc = yc_ref[...].astype(BF16)
    pa = jnp.dot(ya, wa_ref[...], preferred_element_type=F32)
    pb = jnp.dot(yb, wb_ref[...], preferred_element_type=F32)
    pc = jnp.dot(yc, wc_ref[...], preferred_element_type=F32)
    halves = []
    for k, (ga, gb, gc) in enumerate(((ga0, gb0, gc0), (ga1, gb1, gc1))):
        sl = slice(k * 512, (k + 1) * 512)
        m = jax.nn.sigmoid(ga[...]) * pa[:, sl]
        m = m + jax.nn.sigmoid(gb[...]) * pb[:, sl]
        m = m + jax.nn.sigmoid(gc[...]) * pc[:, sl]
        halves.append(m.astype(BF16))
    m = jnp.concatenate(halves, axis=1)
    xn = x_ref[...] + jnp.dot(m, wo_ref[...], preferred_element_type=F32)
    xo_ref[...] = xn
    h2_ref[...] = _rmsnorm(xn, gn_ref[...]).astype(BF16)


def _router_kernel(x_ref, gn_ref, wr_ref, g_ref):
    h = _rmsnorm(x_ref[...], gn_ref[...])
    logits = jnp.dot(h, wr_ref[...], preferred_element_type=F32, precision=lax.Precision.HIGHEST)
    lane = lax.broadcasted_iota(jnp.int32, logits.shape, 1)
    neg = jnp.float32(-jnp.inf)
    logits = jnp.where(lane < N_EXPERTS, logits, neg)
    v1 = jnp.max(logits, axis=-1, keepdims=True)
    i1 = jnp.min(jnp.where(logits == v1, lane, LANES), axis=-1, keepdims=True)
    rest = jnp.where(lane == i1, neg, logits)
    v2 = jnp.max(rest, axis=-1, keepdims=True)
    i2 = jnp.min(jnp.where(rest == v2, lane, LANES), axis=-1, keepdims=True)
    e2 = jnp.exp(v2 - v1)
    den = 1.0 + e2
    g_ref[...] = jnp.where(lane == i1, 1.0 / den, 0.0) + jnp.where(lane == i2, e2 / den, 0.0)


def _router(x, g, router_pad):
    tm = 1280
    return pl.pallas_call(
        _router_kernel,
        grid=(N_TOK // tm,),
        in_specs=[
            pl.BlockSpec((tm, D_MODEL), lambda i: (i, 0)),
            pl.BlockSpec((1, D_MODEL), lambda i: (0, 0)),
            pl.BlockSpec((D_MODEL, LANES), lambda i: (0, 0)),
        ],
        out_specs=pl.BlockSpec((tm, LANES), lambda i: (i, 0)),
        out_shape=jax.ShapeDtypeStruct((N_TOK, LANES), F32),
        compiler_params=_cparams(("parallel",)),
        name="router",
    )(x, g.reshape(1, D_MODEL), router_pad)


def _ffn_kernel(h_ref, x_ref, gate_ref, w1_ref, w3_ref, w2_ref, gf_ref, o_ref, acc_scr, *,
                gated, final_norm):
    e = pl.program_id(1)
    f = pl.program_id(2)

    @pl.when((e == 0) & (f == 0))
    def _():
        acc_scr[...] = jnp.zeros_like(acc_scr)

    h = h_ref[...]
    a = jnp.dot(h, w1_ref[...], preferred_element_type=F32)
    b = jnp.dot(h, w3_ref[...], preferred_element_type=F32)
    t = (a * jax.nn.sigmoid(a)) * b
    c = jnp.dot(t.astype(BF16), w2_ref[...], preferred_element_type=F32)
    if gated:
        lane = lax.broadcasted_iota(jnp.int32, gate_ref.shape, 1)
        c = c * jnp.sum(jnp.where(lane == e, gate_ref[...], 0.0), axis=-1, keepdims=True)
    acc_scr[...] += c

    @pl.when((e == pl.num_programs(1) - 1) & (f == pl.num_programs(2) - 1))
    def _():
        y = x_ref[...] + acc_scr[...]
        if final_norm:
            y = _rmsnorm(y, gf_ref[...])
        o_ref[...] = y


def _ffn(h2, x, gates, w1, w3, w2, g_final, *, tf, gated, final_norm):
    tm = 1280
    n_e, _, d_ff = w1.shape
    kern = functools.partial(_ffn_kernel, gated=gated, final_norm=final_norm)
    return pl.pallas_call(
        kern,
        grid=(N_TOK // tm, n_e, d_ff // tf),
        in_specs=[
            pl.BlockSpec((tm, D_MODEL), lambda i, e, f: (i, 0)),
            pl.BlockSpec((tm, D_MODEL), lambda i, e, f: (i, 0)),
            pl.BlockSpec((tm, LANES), lambda i, e, f: (i, 0)),
            pl.BlockSpec((None, D_MODEL, tf), lambda i, e, f: (e, 0, f)),
            pl.BlockSpec((None, D_MODEL, tf), lambda i, e, f: (e, 0, f)),
            pl.BlockSpec((None, tf, D_MODEL), lambda i, e, f: (e, f, 0)),
            pl.BlockSpec((1, D_MODEL), lambda i, e, f: (0, 0)),
        ],
        out_specs=pl.BlockSpec((tm, D_MODEL), lambda i, e, f: (i, 0)),
        out_shape=jax.ShapeDtypeStruct((N_TOK, D_MODEL), F32),
        scratch_shapes=[pltpu.VMEM((tm, D_MODEL), F32)],
        compiler_params=_cparams(("parallel", "arbitrary", "arbitrary")),
        name="ffn_moe" if gated else "ffn_dense",
    )(h2, x, gates, w1, w3, w2, g_final.reshape(1, D_MODEL))


def _pad_hist(hist, width):
    nseq, _, c = hist.shape
    return jnp.concatenate([jnp.zeros((nseq, SUBLANES - (width - 1), c), hist.dtype), hist], axis=1)


def _block_diag(w):
    h, d, _ = w.shape
    eye = jnp.eye(h, dtype=w.dtype)
    return (eye[:, None, :, None] * w[:, :, None, :]).reshape(h * d, h * d)


def kernel(x_prompt, x_sample, cache_k, cache_v, state_conv_a, state_conv_b, state_h, page_table,
           norm_mix, w_in, conv_a_w, conv_b_w, conv_b_bias, rg_w_r, rg_b_r, rg_w_i, rg_b_i,
           rg_lambda, sb_bias, w_br_a, w_br_b, w_br_c, w_out, norm_ffn, ffn_w1, ffn_w3, ffn_w2,
           moe_router, moe_w1, moe_w3, moe_w2, norm_final):
    depth = w_in.shape[0]
    n_phys = cache_k.shape[1]
    cache_k4 = cache_k.transpose(0, 1, 3, 4, 2).reshape(depth, n_phys, W_C, PAGE_SIZE)
    cache_v4 = cache_v.transpose(0, 1, 3, 4, 2).reshape(depth, n_phys, W_C, PAGE_SIZE)
    x = jnp.concatenate([x_prompt.reshape(N_PROMPT, D_MODEL), x_sample.reshape(N_SAMPLE, D_MODEL)])
    ri = lax.broadcasted_iota(jnp.int32, (ATT_TQ, ATT_TQ), 0)
    ci = lax.broadcasted_iota(jnp.int32, (ATT_TQ, ATT_TQ), 1)
    u = (ri > ci).astype(BF16)
    zeros8 = jnp.zeros((BATCH, SUBLANES, W_A), F32)
    gates_one = jnp.ones((N_TOK, LANES), F32)

    st_p, st_s = [], []
    for l in range(depth):
        lw = {
            "conv_a_w": conv_a_w[l], "conv_b_w": conv_b_w[l], "conv_b_bias": conv_b_bias[l][None],
            "rg_w_r": _block_diag(rg_w_r[l]).astype(BF16), "rg_b_r": rg_b_r[l].reshape(1, W_B),
            "rg_w_i": _block_diag(rg_w_i[l]).astype(BF16), "rg_b_i": rg_b_i[l].reshape(1, W_B),
            "rg_lambda": rg_lambda[l][None],
            "w_br_a": w_br_a[l].astype(BF16), "w_br_b": w_br_b[l].astype(BF16),
            "w_br_c": w_br_c[l].astype(BF16), "w_out": w_out[l].astype(BF16),
            "norm_ffn": norm_ffn[l][None],
        }
        proj = _in_proj(x, norm_mix[l], w_in[l].astype(BF16))

        y_ab, sa_p, sb_p, hl_p = _mixers(proj, zeros8, zeros8, zeros8, lw, nseq=BATCH, t=SEQ,
                                         row0=0, first_pos_zero=True)
        h0_8 = jnp.broadcast_to(state_h[l][:, None, :], (DEC_BATCH, SUBLANES, W_B))
        y_ab, sa_s, sb_s, hl_s = _mixers(proj, _pad_hist(state_conv_a[l], CONV_A),
                                         _pad_hist(state_conv_b[l], CONV_B), h0_8, lw,
                                         nseq=DEC_BATCH, t=DEC_SEQ, row0=N_PROMPT,
                                         first_pos_zero=False, y_prev=y_ab)

        y_c = _attn_prompt(proj, sb_bias[l], u)
        y_c = _attn_sample(proj, y_c, cache_k4, cache_v4, page_table, l, sb_bias[l], u)

        x, h2 = _merge(y_ab, y_c, proj, x, lw)

        last = l == depth - 1
        if l % 2 == 0:
            i = l // 2
            x = _ffn(h2, x, gates_one, ffn_w1[i][None].astype(BF16), ffn_w3[i][None].astype(BF16),
                     ffn_w2[i][None].astype(BF16), norm_final, tf=256, gated=False, final_norm=last)
        else:
            i = l // 2
            router_pad = jnp.pad(moe_router[i], ((0, 0), (0, LANES - N_EXPERTS)))
            gates = _router(x, norm_ffn[l], router_pad)
            x = _ffn(h2, x, gates, moe_w1[i].astype(BF16), moe_w3[i].astype(BF16),
                     moe_w2[i].astype(BF16), norm_final, tf=512, gated=True, final_norm=last)

        kp = proj[:N_PROMPT, COL_K:COL_K + W_C].reshape(BATCH, SEQ, H_C, DH_C)
        vp = proj[:N_PROMPT, COL_V:COL_V + W_C].reshape(BATCH, SEQ, H_C, DH_C)
        ks = proj[N_PROMPT:, COL_K:COL_K + W_C].reshape(DEC_BATCH, DEC_SEQ, H_C, DH_C)
        vs = proj[N_PROMPT:, COL_V:COL_V + W_C].reshape(DEC_BATCH, DEC_SEQ, H_C, DH_C)
        st_p.append((kp, vp, sa_p[:, SUBLANES - (CONV_A - 1):], sb_p[:, SUBLANES - (CONV_B - 1):],
                     hl_p[:, 0]))
        st_s.append((ks, vs, sa_s[:, SUBLANES - (CONV_A - 1):], sb_s[:, SUBLANES - (CONV_B - 1):],
                     hl_s[:, 0]))

    y_prompt = x[:N_PROMPT].reshape(BATCH, SEQ, D_MODEL)
    y_sample = x[N_PROMPT:].reshape(DEC_BATCH, DEC_SEQ, D_MODEL)
    stack = lambda sts, k: jnp.stack([s[k] for s in sts])
    return (y_prompt, y_sample,
            stack(st_p, 0), stack(st_p, 1), stack(st_p, 2), stack(st_p, 3), stack(st_p, 4),
            stack(st_s, 0), stack(st_s, 1), stack(st_s, 2), stack(st_s, 3), stack(st_s, 4))
```

```python
import functools

import jax
import jax.numpy as jnp
from jax import lax
from jax.experimental import pallas as pl
from jax.experimental.pallas import tpu as pltpu

F32 = jnp.float32
BF16 = jnp.bfloat16

D_MODEL = 1024
BATCH = 4
SEQ = 4096
DEC_BATCH = 32
DEC_SEQ = 8
PAGE_SIZE = 128
W_A = 256
W_B = 256
H_B = 4
DH_B = 64
CONV_A = 3
CONV_B = 4
RG_C = 8.0
H_C = 8
DH_C = 64
W_C = 512
IN_WIDTH = 5632
N_EXPERTS = 8
EPS = 1e-6

N_PROMPT = BATCH * SEQ
N_SAMPLE = DEC_BATCH * DEC_SEQ
N_TOK = N_PROMPT + N_SAMPLE

COL_Q = 1024
COL_K = 1536
COL_V = 2048
COL_G = 2560

SUBLANES = 8
LANES = 128
VMEM_LIMIT = 56 * 1024 * 1024

ATT_TQ = 512
ATT_TK = 256
PAGES_PER_STEP = 16

MOE_TM = 256
MOE_T = 512
MOE_TF = 512
MOE_ROWS = -(-(2 * N_TOK + N_EXPERTS * (MOE_T - 1)) // MOE_T) * MOE_T


def _cparams(sem):
    return pltpu.CompilerParams(dimension_semantics=sem, vmem_limit_bytes=VMEM_LIMIT)


def _rmsnorm(x, g):
    y = x * lax.rsqrt(jnp.mean(x * x, axis=-1, keepdims=True) + EPS)
    return y * g


def _inproj_kernel(x_ref, g_ref, w_ref, o_ref, h_scr):
    @pl.when(pl.program_id(1) == 0)
    def _():
        h_scr[...] = _rmsnorm(x_ref[...], g_ref[...]).astype(BF16)

    o_ref[...] = jnp.dot(h_scr[...], w_ref[...], preferred_element_type=F32)


def _in_proj(x, g, w_bf16):
    tm, tn = 1280, 512
    return pl.pallas_call(
        _inproj_kernel,
        grid=(N_TOK // tm, IN_WIDTH // tn),
        in_specs=[
            pl.BlockSpec((tm, D_MODEL), lambda i, j: (i, 0)),
            pl.BlockSpec((1, D_MODEL), lambda i, j: (0, 0)),
            pl.BlockSpec((D_MODEL, tn), lambda i, j: (0, j)),
        ],
        out_specs=pl.BlockSpec((tm, tn), lambda i, j: (i, j)),
        out_shape=jax.ShapeDtypeStruct((N_TOK, IN_WIDTH), F32),
        scratch_shapes=[pltpu.VMEM((tm, D_MODEL), BF16)],
        compiler_params=_cparams(("parallel", "arbitrary")),
        name="in_proj",
    )(x, g.reshape(1, D_MODEL), w_bf16)


def _softplus(x):
    return jnp.maximum(x, 0.0) + jnp.log1p(jnp.exp(-jnp.abs(x)))


def _mixers_kernel(p_ref, ha_ref, hb_ref, h0_ref, wa_ref, wb_ref, bb_ref, wr_ref, br_ref,
                   wi_ref, bi_ref, lam_ref, y_prev_hbm, y_ref, sa_ref, sb_ref, hl_ref, ua_buf, xb_buf,
                   hc_scr, *, tt, first_pos_zero):
    del y_prev_hbm
    ti = pl.program_id(1)

    @pl.when(ti == 0)
    def _():
        ua_buf[0:SUBLANES, :] = ha_ref[...]
        xb_buf[0:SUBLANES, :] = hb_ref[...]
        hc_scr[...] = h0_ref[...]

    bg = p_ref[:, 0:W_A]
    cg = p_ref[:, W_A:2 * W_A]
    xa = p_ref[:, 2 * W_A:3 * W_A]
    xb = p_ref[:, 3 * W_A:3 * W_A + W_B]

    ua = cg * xa
    ua_buf[SUBLANES:SUBLANES + tt, :] = ua
    ya = ua_buf[pl.ds(SUBLANES - 2, tt), :] * wa_ref[0:1, :]
    ya = ya + ua_buf[pl.ds(SUBLANES - 1, tt), :] * wa_ref[1:2, :]
    ya = ya + ua * wa_ref[2:3, :]
    y_ref[:, 0:W_A] = bg * ya

    xb_buf[SUBLANES:SUBLANES + tt, :] = xb
    xc = xb_buf[pl.ds(SUBLANES - 3, tt), :] * wb_ref[0:1, :]
    xc = xc + xb_buf[pl.ds(SUBLANES - 2, tt), :] * wb_ref[1:2, :]
    xc = xc + xb_buf[pl.ds(SUBLANES - 1, tt), :] * wb_ref[2:3, :]
    xc = xc + xb * wb_ref[3:4, :]
    xc = xc + bb_ref[...]
    xcb = xc.astype(BF16)
    r = jax.nn.sigmoid(jnp.dot(xcb, wr_ref[...], preferred_element_type=F32) + br_ref[...])
    ig = jax.nn.sigmoid(jnp.dot(xcb, wi_ref[...], preferred_element_type=F32) + bi_ref[...])
    log_a = (-RG_C * r) * _softplus(-lam_ref[...])
    a = jnp.exp(log_a)
    mult = jnp.sqrt(1.0 - jnp.exp(2.0 * log_a))
    row = lax.broadcasted_iota(jnp.int32, (tt, W_B), 0)
    if first_pos_zero:
        mult = jnp.where((row == 0) & (ti == 0), 1.0, mult)
    b = mult * (ig * xc)

    s = 1
    while s < tt:
        keep = row >= s
        a_sh = pltpu.roll(a, s, 0)
        b_sh = pltpu.roll(b, s, 0)
        b = jnp.where(keep, a * b_sh + b, b)
        a = jnp.where(keep, a * a_sh, a)
        s *= 2
    h = a * hc_scr[0:1, :] + b
    y_ref[:, W_A:W_A + W_B] = h

    last8_a = ua_buf[tt:tt + SUBLANES, :]
    last8_b = xb_buf[tt:tt + SUBLANES, :]
    hlast = jnp.broadcast_to(h[tt - 1:tt, :], (SUBLANES, W_B))
    ua_buf[0:SUBLANES, :] = last8_a
    xb_buf[0:SUBLANES, :] = last8_b
    hc_scr[...] = hlast
    sa_ref[...] = last8_a
    sb_ref[...] = last8_b
    hl_ref[...] = hlast


def _mixers(proj, hist_a8, hist_b8, h0_8, lw, y_prev, *, nseq, t, row0, first_pos_zero):
    tt = min(t, 512)
    nt = t // tt
    rb0 = row0 // tt
    kern = functools.partial(_mixers_kernel, tt=tt, first_pos_zero=first_pos_zero)
    seq3 = pl.BlockSpec((None, SUBLANES, W_A), lambda s, i: (s, 0, 0))
    full = lambda shape: pl.BlockSpec(shape, lambda s, i: (0,) * len(shape))
    in_specs = [
        pl.BlockSpec((tt, 4 * W_A), lambda s, i: (rb0 + s * nt + i, 0)),
        seq3, seq3, seq3,
        full((CONV_A, W_A)), full((CONV_B, W_B)), full((1, W_B)),
        full((W_B, W_B)), full((1, W_B)), full((W_B, W_B)), full((1, W_B)), full((1, W_B)),
        pl.BlockSpec(memory_space=pl.ANY),
    ]
    args = [proj, hist_a8, hist_b8, h0_8, lw["conv_a_w"], lw["conv_b_w"], lw["conv_b_bias"],
            lw["rg_w_r"], lw["rg_b_r"], lw["rg_w_i"], lw["rg_b_i"], lw["rg_lambda"], y_prev]
    st = jax.ShapeDtypeStruct((nseq, SUBLANES, W_A), F32)
    return pl.pallas_call(
        kern,
        grid=(nseq, nt),
        in_specs=in_specs,
        out_specs=[
            pl.BlockSpec((tt, W_A + W_B), lambda s, i: (rb0 + s * nt + i, 0)),
            seq3, seq3, seq3,
        ],
        out_shape=[jax.ShapeDtypeStruct((N_TOK, W_A + W_B), F32), st, st, st],
        scratch_shapes=[
            pltpu.VMEM((SUBLANES + tt, W_A), F32),
            pltpu.VMEM((SUBLANES + tt, W_B), F32),
            pltpu.VMEM((SUBLANES, W_B), F32),
        ],
        input_output_aliases={len(args) - 1: 0},
        compiler_params=_cparams(("parallel", "arbitrary")),
        name="mixers_t%d" % t,
    )(*args)


def _log_sigmoid_pair(z):
    ls = jnp.minimum(z, 0.0) - jnp.log(1.0 + jnp.exp(-jnp.abs(z)))
    return ls, ls - z


def _split_bf16(x):
    hi = x.astype(BF16)
    lo = (x - hi.astype(F32)).astype(BF16)
    return hi, lo


def _sb_chunk(qm, bias, kc, vc, u, carry, mask=None):
    z = lax.dot_general(qm, kc, (((1,), (1,)), ((), ())), preferred_element_type=F32) + bias
    ls, lk = _log_sigmoid_pair(z)
    if mask is not None:
        lk = jnp.where(mask, lk, 0.0)
    lk_hi, lk_lo = _split_bf16(lk)
    after = (jnp.dot(lk_hi, u, preferred_element_type=F32)
             + jnp.dot(lk_lo, u, preferred_element_type=F32))
    total = after[:, 0:1] + lk[:, 0:1]
    w = jnp.exp(ls + after + carry)
    if mask is not None:
        w = jnp.where(mask, w, 0.0)
    return jnp.dot(w.astype(BF16), vc, preferred_element_type=F32), total


def _attn_prompt_kernel(bias_ref, q_ref, k_ref, v_ref, u_ref, y_prev_hbm, o_ref, kb_scr, vb_scr,
                        acc_scr, cr_scr):
    del y_prev_hbm
    hp = pl.program_id(1)
    qi = pl.program_id(2)
    tq, tk = ATT_TQ, ATT_TK
    nblk = tq // tk

    @pl.when(qi == 0)
    def _():
        kb_scr[...] = k_ref[...].astype(BF16)
        vb_scr[...] = v_ref[...].astype(BF16)

    q = q_ref[...] * (DH_C ** -0.5)
    lane = lax.broadcasted_iota(jnp.int32, (tq, LANES), 1)
    qm = [jnp.where(lane < DH_C, q, 0.0).astype(BF16), jnp.where(lane >= DH_C, q, 0.0).astype(BF16)]
    bias = [bias_ref[2 * hp], bias_ref[2 * hp + 1]]
    u = u_ref[...]
    row = lax.broadcasted_iota(jnp.int32, (tq, tk), 0)
    col = lax.broadcasted_iota(jnp.int32, (tq, tk), 1)

    def block(blk, hh, first, mask):
        st = pl.multiple_of(blk * tk, tk)
        carry = 0.0 if first else cr_scr[hh, :, 0:1]
        pv, total = _sb_chunk(qm[hh], bias[hh], kb_scr[pl.ds(st, tk), :], vb_scr[pl.ds(st, tk), :],
                              u, carry, mask)
        total = jnp.broadcast_to(total, (tq, LANES))
        if first:
            acc_scr[hh] = pv
            cr_scr[hh] = total
        else:
            acc_scr[hh] += pv
            cr_scr[hh] += total

    for d in range(nblk - 1, -1, -1):
        causal = col + d * tk < row
        for hh in range(2):
            block(qi * nblk + d, hh, d == nblk - 1, causal)

    def body(jj, _):
        for hh in range(2):
            block(qi * nblk - jj, hh, False, None)
        return 0

    lax.fori_loop(1, qi * nblk + 1, body, 0)
    o_ref[...] = jnp.where(lane < DH_C, acc_scr[0], acc_scr[1])


def _attn_prompt(proj, sb_bias, u, y_prev):
    tq = ATT_TQ
    nq = SEQ // tq
    grid_spec = pltpu.PrefetchScalarGridSpec(
        num_scalar_prefetch=1,
        grid=(BATCH, H_C // 2, nq),
        in_specs=[
            pl.BlockSpec((tq, LANES), lambda b, hp, qi, s: (b * nq + qi, COL_Q // LANES + hp)),
            pl.BlockSpec((SEQ, LANES), lambda b, hp, qi, s: (b, COL_K // LANES + hp)),
            pl.BlockSpec((SEQ, LANES), lambda b, hp, qi, s: (b, COL_V // LANES + hp)),
            pl.BlockSpec((ATT_TK, ATT_TK), lambda b, hp, qi, s: (0, 0)),
            pl.BlockSpec(memory_space=pl.ANY),
        ],
        out_specs=pl.BlockSpec((tq, LANES), lambda b, hp, qi, s: (b * nq + qi, hp)),
        scratch_shapes=[
            pltpu.VMEM((SEQ, LANES), BF16),
            pltpu.VMEM((SEQ, LANES), BF16),
            pltpu.VMEM((2, tq, LANES), F32),
            pltpu.VMEM((2, tq, LANES), F32),
        ],
    )
    return pl.pallas_call(
        _attn_prompt_kernel,
        grid_spec=grid_spec,
        out_shape=jax.ShapeDtypeStruct((N_TOK, W_C), F32),
        input_output_aliases={5: 0},
        compiler_params=_cparams(("parallel", "parallel", "arbitrary")),
        name="attn_prompt",
    )(sb_bias, proj, proj, proj, u, y_prev)


def _attn_sample_kernel(pt_ref, bias_ref, q_ref, kn_ref, vn_ref, u_ref, *rest):
    npg = PAGES_PER_STEP
    k_refs = rest[:npg]
    v_refs = rest[npg:2 * npg]
    (o_ref, qbd_scr, bias_scr, kn_scr, vn_scr, kcat_scr, vcat_scr, acc_scr,
     cr_scr) = rest[2 * npg + 1:]
    j = pl.program_id(1)
    nrow = H_C * DEC_SEQ
    u = u_ref[0:PAGE_SIZE, 0:PAGE_SIZE]

    @pl.when(j == 0)
    def _():
        row = lax.broadcasted_iota(jnp.int32, (nrow, W_C), 0)
        lane = lax.broadcasted_iota(jnp.int32, (nrow, W_C), 1)
        q8 = q_ref[...] * (DH_C ** -0.5)
        qt = jnp.concatenate([q8] * H_C, axis=0)
        qbd_scr[...] = jnp.where(row // DEC_SEQ == lane // DH_C, qt, 0.0).astype(BF16)
        rowb = lax.broadcasted_iota(jnp.int32, (nrow, LANES), 0)
        bias = jnp.zeros((nrow, LANES), F32)
        for h in range(H_C):
            bias = jnp.where(rowb // DEC_SEQ == h, bias_ref[h], bias)
        bias_scr[...] = bias
        kn_scr[...] = jnp.zeros((PAGE_SIZE, W_C), BF16)
        vn_scr[...] = jnp.zeros((PAGE_SIZE, W_C), BF16)
        kn_scr[0:DEC_SEQ, :] = kn_ref[...].astype(BF16)
        vn_scr[0:DEC_SEQ, :] = vn_ref[...].astype(BF16)
        rowc = lax.broadcasted_iota(jnp.int32, (nrow, PAGE_SIZE), 0)
        colc = lax.broadcasted_iota(jnp.int32, (nrow, PAGE_SIZE), 1)
        pv, total = _sb_chunk(qbd_scr[...], bias_scr[:, 0:1], kn_scr[...], vn_scr[...], u, 0.0,
                              mask=colc < rowc % DEC_SEQ)
        acc_scr[...] = pv
        cr_scr[...] = jnp.broadcast_to(total, (nrow, LANES))

    for p in range(npg):
        kcat_scr[:, p * PAGE_SIZE:(p + 1) * PAGE_SIZE] = k_refs[p][...].astype(BF16)
        vcat_scr[:, p * PAGE_SIZE:(p + 1) * PAGE_SIZE] = v_refs[p][...].astype(BF16)
    z = jnp.dot(qbd_scr[...], kcat_scr[...], preferred_element_type=F32)
    ls, lk = _log_sigmoid_pair(z + bias_scr[:, 0:1])
    lk_hi, lk_lo = _split_bf16(lk)
    carry = cr_scr[:, 0:1]
    afters = []
    for p in range(npg):
        sl = slice(p * PAGE_SIZE, (p + 1) * PAGE_SIZE)
        a = (jnp.dot(lk_hi[:, sl], u, preferred_element_type=F32)
             + jnp.dot(lk_lo[:, sl], u, preferred_element_type=F32))
        afters.append(a + carry)
        carry = carry + (a[:, 0:1] + lk[:, p * PAGE_SIZE:p * PAGE_SIZE + 1])
    w = jnp.exp(ls + jnp.concatenate(afters, axis=1)).astype(BF16)
    acc_scr[...] += lax.dot_general(w, vcat_scr[...], (((1,), (1,)), ((), ())),
                                    preferred_element_type=F32)
    cr_scr[...] = jnp.broadcast_to(carry, (nrow, LANES))

    @pl.when(j == pl.num_programs(1) - 1)
    def _():
        lane = lax.broadcasted_iota(jnp.int32, (DEC_SEQ, W_C), 1)
        out = jnp.zeros((DEC_SEQ, W_C), F32)
        for h in range(H_C):
            out = out + jnp.where(lane // DH_C == h, acc_scr[h * DEC_SEQ:(h + 1) * DEC_SEQ, :], 0.0)
        o_ref[...] = out


def _attn_sample(proj, y_c, cache_k4, cache_v4, page_table, layer, sb_bias, u):
    npg = PAGES_PER_STEP
    n_pages = page_table.shape[1]
    nsteps = n_pages // npg
    rb0 = N_PROMPT // DEC_SEQ
    nrow = H_C * DEC_SEQ

    def row_spec(colblk):
        return pl.BlockSpec((DEC_SEQ, W_C), lambda b, j, pt, s: (rb0 + b, colblk))

    def page_spec(p):
        return pl.BlockSpec((None, None, W_C, PAGE_SIZE),
                            lambda b, j, pt, s: (layer, pt[b, n_pages - 1 - (j * npg + p)], 0, 0))

    grid_spec = pltpu.PrefetchScalarGridSpec(
        num_scalar_prefetch=2,
        grid=(DEC_BATCH, nsteps),
        in_specs=([row_spec(COL_Q // W_C), row_spec(COL_K // W_C), row_spec(COL_V // W_C),
                   pl.BlockSpec((ATT_TK, ATT_TK), lambda b, j, pt, s: (0, 0))]
                  + [page_spec(p) for p in range(npg)]
                  + [page_spec(p) for p in range(npg)]
                  + [pl.BlockSpec(memory_space=pl.ANY)]),
        out_specs=pl.BlockSpec((DEC_SEQ, W_C), lambda b, j, pt, s: (rb0 + b, 0)),
        scratch_shapes=[
            pltpu.VMEM((nrow, W_C), BF16),
            pltpu.VMEM((nrow, LANES), F32),
            pltpu.VMEM((PAGE_SIZE, W_C), BF16),
            pltpu.VMEM((PAGE_SIZE, W_C), BF16),
            pltpu.VMEM((W_C, npg * PAGE_SIZE), BF16),
            pltpu.VMEM((W_C, npg * PAGE_SIZE), BF16),
            pltpu.VMEM((nrow, W_C), F32),
            pltpu.VMEM((nrow, LANES), F32),
        ],
    )
    n_in = 2 + 4 + 2 * npg
    return pl.pallas_call(
        _attn_sample_kernel,
        grid_spec=grid_spec,
        out_shape=jax.ShapeDtypeStruct((N_TOK, W_C), F32),
        input_output_aliases={n_in: 0},
        compiler_params=_cparams(("parallel", "arbitrary")),
        name="attn_sample",
    )(page_table, sb_bias, proj, proj, proj, u, *([cache_k4] * npg), *([cache_v4] * npg), y_c)


def _merge(y_ab, y_c, proj, x, lw):
    tm = 640
    rows = lambda w, c: pl.BlockSpec((tm, w), lambda i: (i, c))
    full = lambda shape: pl.BlockSpec(shape, lambda i: (0, 0))
    return pl.pallas_call(
        _merge_kernel_split,
        grid=(N_TOK // tm,),
        in_specs=[
            rows(W_A + W_B, 0), rows(W_C, 0),
            *[pl.BlockSpec((tm, 512), functools.partial(lambda i, c: (i, c), c=COL_G // 512 + k))
              for k in range(6)],
            rows(D_MODEL, 0),
            full((W_A, D_MODEL)), full((W_B, D_MODEL)), full((W_C, D_MODEL)),
            full((D_MODEL, D_MODEL)), full((1, D_MODEL)),
        ],
        out_specs=[rows(D_MODEL, 0), rows(D_MODEL, 0)],
        out_shape=[jax.ShapeDtypeStruct((N_TOK, D_MODEL), F32),
                   jax.ShapeDtypeStruct((N_TOK, D_MODEL), BF16)],
        compiler_params=_cparams(("parallel",)),
        name="merge",
    )(y_ab, y_c, proj, proj, proj, proj, proj, proj, x, lw["w_br_a"], lw["w_br_b"], lw["w_br_c"],
      lw["w_out"], lw["norm_ffn"])


def _merge_kernel_split(yab_ref, yc_ref, ga0, ga1, gb0, gb1, gc0, gc1, x_ref, wa_ref, wb_ref,
                        wc_ref, wo_ref, gn_ref, xo_ref, h2_ref):
    ya = yab_ref[:, 0:W_A].astype(BF16)
    yb = yab_ref[:, W_A:W_A + W_B].astype(BF16)
    yc = yc_ref[...].astype(BF16)
    pa = jnp.dot(ya, wa_ref[...], preferred_element_type=F32)
    pb = jnp.dot(yb, wb_ref[...], preferred_element_type=F32)
    pc = jnp.dot(yc, wc_ref[...], preferred_element_type=F32)
    halves = []
    for k, (ga, gb, gc) in enumerate(((ga0, gb0, gc0), (ga1, gb1, gc1))):
        sl = slice(k * 512, (k + 1) * 512)
        m = jax.nn.sigmoid(ga[...]) * pa[:, sl]
        m = m + jax.nn.sigmoid(gb[...]) * pb[:, sl]
        m = m + jax.nn.sigmoid(gc[...]) * pc[:, sl]
        halves.append(m.astype(BF16))
    m = jnp.concatenate(halves, axis=1)
    xn = x_ref[...] + jnp.dot(m, wo_ref[...], preferred_element_type=F32)
    xo_ref[...] = xn
    h2_ref[...] = _rmsnorm(xn, gn_ref[...]).astype(BF16)


def _router_kernel(x_ref, gn_ref, wr_ref, u_ref, route_ref, cnt_ref, cnt_scr):
    @pl.when(pl.program_id(0) == 0)
    def _():
        cnt_scr[...] = jnp.zeros_like(cnt_scr)

    h = _rmsnorm(x_ref[...], gn_ref[...])
    logits = jnp.dot(h, wr_ref[...], preferred_element_type=F32, precision=lax.Precision.HIGHEST)
    lane = lax.broadcasted_iota(jnp.int32, logits.shape, 1)
    neg = jnp.float32(-jnp.inf)
    logits = jnp.where(lane < N_EXPERTS, logits, neg)
    v1 = jnp.max(logits, axis=-1, keepdims=True)
    i1 = jnp.min(jnp.where(logits == v1, lane, LANES), axis=-1, keepdims=True)
    rest = jnp.where(lane == i1, neg, logits)
    v2 = jnp.max(rest, axis=-1, keepdims=True)
    i2 = jnp.min(jnp.where(rest == v2, lane, LANES), axis=-1, keepdims=True)
    e2 = jnp.exp(v2 - v1)
    den = 1.0 + e2
    oh1 = lane == i1
    oh2 = lane == i2
    m = jnp.where(oh1 | oh2, 1.0, 0.0)
    before = jnp.dot(u_ref[...], m.astype(BF16), preferred_element_type=F32) + cnt_scr[0:1, :]
    r1 = jnp.sum(jnp.where(oh1, before, 0.0), axis=-1, keepdims=True)
    r2 = jnp.sum(jnp.where(oh2, before, 0.0), axis=-1, keepdims=True)
    cols = (1.0 / den, e2 / den, i1.astype(F32), i2.astype(F32), r1, r2)
    route = jnp.zeros(logits.shape, F32)
    for k, c in enumerate(cols):
        route = jnp.where(lane == k, c, route)
    route_ref[...] = route
    cnt = jnp.broadcast_to(cnt_scr[0:1, :] + jnp.sum(m, axis=0, keepdims=True), cnt_scr.shape)
    cnt_scr[...] = cnt
    cnt_ref[...] = cnt


def _router(x, g, router_pad, u):
    tm = MOE_TM
    return pl.pallas_call(
        _router_kernel,
        grid=(N_TOK // tm,),
        in_specs=[
            pl.BlockSpec((tm, D_MODEL), lambda i: (i, 0)),
            pl.BlockSpec((1, D_MODEL), lambda i: (0, 0)),
            pl.BlockSpec((D_MODEL, LANES), lambda i: (0, 0)),
            pl.BlockSpec((tm, tm), lambda i: (0, 0)),
        ],
        out_specs=[pl.BlockSpec((tm, LANES), lambda i: (i, 0)),
                   pl.BlockSpec((SUBLANES, LANES), lambda i: (0, 0))],
        out_shape=[jax.ShapeDtypeStruct((N_TOK, LANES), F32),
                   jax.ShapeDtypeStruct((SUBLANES, LANES), F32)],
        scratch_shapes=[pltpu.VMEM((SUBLANES, LANES), F32)],
        compiler_params=_cparams(("arbitrary",)),
        name="router",
    )(x, g.reshape(1, D_MODEL), router_pad, u)


def _route_plan(route, cnt):
    t = MOE_T
    e12 = route[:, 2:4].astype(jnp.int32)
    rank = route[:, 4:6].astype(jnp.int32)
    counts = cnt[0, :N_EXPERTS].astype(jnp.int32)
    padded = ((counts + t - 1) // t) * t
    ends = jnp.cumsum(padded)
    pos = (ends - padded)[e12] + rank
    n_active = ends[-1] // t
    tile_start = jnp.arange(MOE_ROWS // t, dtype=jnp.int32) * t
    tile_expert = jnp.sum(tile_start[:, None] >= ends[None, :], axis=1).astype(jnp.int32)
    last_expert = jnp.sum((n_active - 1) * t >= ends).astype(jnp.int32)
    tile_expert = jnp.minimum(tile_expert, last_expert)
    pos_blk = pos.reshape(N_TOK // MOE_TM, MOE_TM, 2).transpose(0, 2, 1)
    return pos_blk, tile_expert, n_active.reshape(1).astype(jnp.int32)


def _dispatch_kernel(pos_ref, x3_hbm, xs_in_hbm, xs_hbm, sem):
    del xs_in_hbm
    base = pl.program_id(0) * MOE_TM

    def issue(r, c):
        src = x3_hbm.at[base + r]
        pltpu.make_async_copy(src, xs_hbm.at[pos_ref[0, r]], sem).start()
        pltpu.make_async_copy(src, xs_hbm.at[pos_ref[1, r]], sem).start()
        return c

    def drain(r, c):
        pltpu.make_async_copy(x3_hbm.at[0], xs_hbm.at[0], sem).wait()
        pltpu.make_async_copy(x3_hbm.at[0], xs_hbm.at[0], sem).wait()
        return c

    lax.fori_loop(0, MOE_TM, issue, 0)
    lax.fori_loop(0, MOE_TM, drain, 0)


def _dispatch(pos_blk, x3):
    zeros = jnp.zeros((MOE_ROWS, SUBLANES, LANES), F32)
    return pl.pallas_call(
        _dispatch_kernel,
        grid=(N_TOK // MOE_TM,),
        in_specs=[
            pl.BlockSpec((None, 2, MOE_TM), lambda i: (i, 0, 0), memory_space=pltpu.SMEM),
            pl.BlockSpec(memory_space=pl.ANY),
            pl.BlockSpec(memory_space=pl.ANY),
        ],
        out_specs=pl.BlockSpec(memory_space=pl.ANY),
        out_shape=jax.ShapeDtypeStruct((MOE_ROWS, SUBLANES, LANES), F32),
        scratch_shapes=[pltpu.SemaphoreType.DMA(())],
        input_output_aliases={2: 0},
        compiler_params=_cparams(("arbitrary",)),
        name="moe_dispatch",
    )(pos_blk, x3, zeros)


def _moe_ffn_kernel(te_ref, na_ref, xs_ref, gn_ref, w1_ref, w3_ref, w2_ref, ys_ref, hb_scr, acc_scr):
    del te_ref
    f = pl.program_id(1)
    active = pl.program_id(0) < na_ref[0]
    nchunk = D_MODEL // LANES

    @pl.when(active & (f == 0))
    def _():
        parts = [xs_ref[:, s, :] for s in range(nchunk)]
        ss = jnp.sum(parts[0] * parts[0], axis=-1, keepdims=True)
        for p in parts[1:]:
            ss = ss + jnp.sum(p * p, axis=-1, keepdims=True)
        inv = lax.rsqrt(ss * (1.0 / D_MODEL) + EPS)
        for s, p in enumerate(parts):
            sl = slice(s * LANES, (s + 1) * LANES)
            hb_scr[:, sl] = ((p * inv) * gn_ref[:, sl]).astype(BF16)
        acc_scr[...] = jnp.zeros_like(acc_scr)

    @pl.when(active)
    def _():
        h = hb_scr[...]
        a = jnp.dot(h, w1_ref[...], preferred_element_type=F32)
        b = jnp.dot(h, w3_ref[...], preferred_element_type=F32)
        t = (a * jax.nn.sigmoid(a)) * b
        acc_scr[...] += jnp.dot(t.astype(BF16), w2_ref[...], preferred_element_type=F32)

    @pl.when(f == pl.num_programs(1) - 1)
    def _():
        for s in range(nchunk):
            sl = slice(s * LANES, (s + 1) * LANES)
            ys_ref[:, s, :] = jnp.where(active, acc_scr[:, sl], 0.0)


def _moe_ffn(tile_expert, n_active, xs3, g, w1, w3, w2):
    t, tf = MOE_T, MOE_TF
    nf = w1.shape[2] // tf

    def wspec(shape, order):
        def index(r, f, te, na):
            ff = jnp.where(r < na[0], f, nf - 1)
            return (te[r], ff, 0) if order else (te[r], 0, ff)
        return pl.BlockSpec(shape, index)

    grid_spec = pltpu.PrefetchScalarGridSpec(
        num_scalar_prefetch=2,
        grid=(MOE_ROWS // t, nf),
        in_specs=[
            pl.BlockSpec((t, SUBLANES, LANES), lambda r, f, te, na: (r, 0, 0)),
            pl.BlockSpec((1, D_MODEL), lambda r, f, te, na: (0, 0)),
            wspec((None, D_MODEL, tf), False),
            wspec((None, D_MODEL, tf), False),
            wspec((None, tf, D_MODEL), True),
        ],
        out_specs=pl.BlockSpec((t, SUBLANES, LANES), lambda r, f, te, na: (r, 0, 0)),
        scratch_shapes=[pltpu.VMEM((t, D_MODEL), BF16), pltpu.VMEM((t, D_MODEL), F32)],
    )
    return pl.pallas_call(
        _moe_ffn_kernel,
        grid_spec=grid_spec,
        out_shape=jax.ShapeDtypeStruct((MOE_ROWS, SUBLANES, LANES), F32),
        compiler_params=_cparams(("parallel", "arbitrary")),
        name="moe_ffn",
    )(tile_expert, n_active, xs3, g.reshape(1, D_MODEL), w1, w3, w2)


def _combine_kernel(pos_ref, route_ref, x_ref, gf_ref, ys_hbm, o_ref, y1_buf, y2_buf, sem, *,
                    final_norm):
    def issue(r, c):
        pltpu.make_async_copy(ys_hbm.at[pos_ref[0, r]], y1_buf.at[r], sem).start()
        pltpu.make_async_copy(ys_hbm.at[pos_ref[1, r]], y2_buf.at[r], sem).start()
        return c

    def drain(r, c):
        pltpu.make_async_copy(ys_hbm.at[0], y1_buf.at[0], sem).wait()
        pltpu.make_async_copy(ys_hbm.at[0], y2_buf.at[0], sem).wait()
        return c

    lax.fori_loop(0, MOE_TM, issue, 0)
    lax.fori_loop(0, MOE_TM, drain, 0)
    g1 = route_ref[:, 0:1]
    g2 = route_ref[:, 1:2]
    parts = []
    for s in range(D_MODEL // LANES):
        sl = slice(s * LANES, (s + 1) * LANES)
        parts.append(x_ref[:, sl] + (g1 * y1_buf[:, s, :] + g2 * y2_buf[:, s, :]))
    if final_norm:
        ss = jnp.sum(parts[0] * parts[0], axis=-1, keepdims=True)
        for p in parts[1:]:
            ss = ss + jnp.sum(p * p, axis=-1, keepdims=True)
        inv = lax.rsqrt(ss * (1.0 / D_MODEL) + EPS)
    for s, p in enumerate(parts):
        sl = slice(s * LANES, (s + 1) * LANES)
        o_ref[:, sl] = (p * inv) * gf_ref[:, sl] if final_norm else p


def _combine(pos_blk, route, x, g_final, ys3, *, final_norm):
    tm = MOE_TM
    return pl.pallas_call(
        functools.partial(_combine_kernel, final_norm=final_norm),
        grid=(N_TOK // tm,),
        in_specs=[
            pl.BlockSpec((None, 2, tm), lambda i: (i, 0, 0), memory_space=pltpu.SMEM),
            pl.BlockSpec((tm, LANES), lambda i: (i, 0)),
            pl.BlockSpec((tm, D_MODEL), lambda i: (i, 0)),
            pl.BlockSpec((1, D_MODEL), lambda i: (0, 0)),
            pl.BlockSpec(memory_space=pl.ANY),
        ],
        out_specs=pl.BlockSpec((tm, D_MODEL), lambda i: (i, 0)),
        out_shape=jax.ShapeDtypeStruct((N_TOK, D_MODEL), F32),
        scratch_shapes=[pltpu.VMEM((tm, SUBLANES, LANES), F32), pltpu.VMEM((tm, SUBLANES, LANES), F32),
                        pltpu.SemaphoreType.DMA(())],
        compiler_params=_cparams(("arbitrary",)),
        name="moe_combine",
    )(pos_blk, route, x, g_final.reshape(1, D_MODEL), ys3)


def _ffn_kernel(h_ref, x_ref, w1_ref, w3_ref, w2_ref, gf_ref, o_ref, acc_scr, *, final_norm):
    f = pl.program_id(1)

    @pl.when(f == 0)
    def _():
        acc_scr[...] = jnp.zeros_like(acc_scr)

    h = h_ref[...]
    a = jnp.dot(h, w1_ref[...], preferred_element_type=F32)
    b = jnp.dot(h, w3_ref[...], preferred_element_type=F32)
    t = (a * jax.nn.sigmoid(a)) * b
    acc_scr[...] += jnp.dot(t.astype(BF16), w2_ref[...], preferred_element_type=F32)

    @pl.when(f == pl.num_programs(1) - 1)
    def _():
        y = x_ref[...] + acc_scr[...]
        if final_norm:
            y = _rmsnorm(y, gf_ref[...])
        o_ref[...] = y


def _ffn(h2, x, w1, w3, w2, g_final, *, final_norm):
    tm, tf = 1280, 256
    d_ff = w1.shape[1]
    return pl.pallas_call(
        functools.partial(_ffn_kernel, final_norm=final_norm),
        grid=(N_TOK // tm, d_ff // tf),
        in_specs=[
            pl.BlockSpec((tm, D_MODEL), lambda i, f: (i, 0)),
            pl.BlockSpec((tm, D_MODEL), lambda i, f: (i, 0)),
            pl.BlockSpec((D_MODEL, tf), lambda i, f: (0, f)),
            pl.BlockSpec((D_MODEL, tf), lambda i, f: (0, f)),
            pl.BlockSpec((tf, D_MODEL), lambda i, f: (f, 0)),
            pl.BlockSpec((1, D_MODEL), lambda i, f: (0, 0)),
        ],
        out_specs=pl.BlockSpec((tm, D_MODEL), lambda i, f: (i, 0)),
        out_shape=jax.ShapeDtypeStruct((N_TOK, D_MODEL), F32),
        scratch_shapes=[pltpu.VMEM((tm, D_MODEL), F32)],
        compiler_params=_cparams(("parallel", "arbitrary")),
        name="ffn_dense",
    )(h2, x, w1, w3, w2, g_final.reshape(1, D_MODEL))


def _pad_hist(hist, width):
    nseq, _, c = hist.shape
    return jnp.concatenate([jnp.zeros((nseq, SUBLANES - (width - 1), c), hist.dtype), hist], axis=1)


def _block_diag(w):
    h, d, _ = w.shape
    eye = jnp.eye(h, dtype=w.dtype)
    return (eye[:, None, :, None] * w[:, :, None, :]).reshape(h * d, h * d)


def kernel(x_prompt, x_sample, cache_k, cache_v, state_conv_a, state_conv_b, state_h, page_table,
           norm_mix, w_in, conv_a_w, conv_b_w, conv_b_bias, rg_w_r, rg_b_r, rg_w_i, rg_b_i,
           rg_lambda, sb_bias, w_br_a, w_br_b, w_br_c, w_out, norm_ffn, ffn_w1, ffn_w3, ffn_w2,
           moe_router, moe_w1, moe_w3, moe_w2, norm_final):
    depth = w_in.shape[0]
    n_phys = cache_k.shape[1]
    cache_k4 = cache_k.transpose(0, 1, 3, 4, 2).reshape(depth, n_phys, W_C, PAGE_SIZE)
    cache_v4 = cache_v.transpose(0, 1, 3, 4, 2).reshape(depth, n_phys, W_C, PAGE_SIZE)
    x = jnp.concatenate([x_prompt.reshape(N_PROMPT, D_MODEL), x_sample.reshape(N_SAMPLE, D_MODEL)])
    ri = lax.broadcasted_iota(jnp.int32, (ATT_TK, ATT_TK), 0)
    ci = lax.broadcasted_iota(jnp.int32, (ATT_TK, ATT_TK), 1)
    u = (ri > ci).astype(BF16)
    zeros8 = jnp.zeros((BATCH, SUBLANES, W_A), F32)

    st_p, st_s = [], []
    for l in range(depth):
        lw = {
            "conv_a_w": conv_a_w[l], "conv_b_w": conv_b_w[l], "conv_b_bias": conv_b_bias[l][None],
            "rg_w_r": _block_diag(rg_w_r[l]).astype(BF16), "rg_b_r": rg_b_r[l].reshape(1, W_B),
            "rg_w_i": _block_diag(rg_w_i[l]).astype(BF16), "rg_b_i": rg_b_i[l].reshape(1, W_B),
            "rg_lambda": rg_lambda[l][None],
            "w_br_a": w_br_a[l].astype(BF16), "w_br_b": w_br_b[l].astype(BF16),
            "w_br_c": w_br_c[l].astype(BF16), "w_out": w_out[l].astype(BF16),
            "norm_ffn": norm_ffn[l][None],
        }
        proj = _in_proj(x, norm_mix[l], w_in[l].astype(BF16))

        y_ab, sa_p, sb_p, hl_p = _mixers(proj, zeros8, zeros8, zeros8, lw, jnp.zeros((N_TOK, W_C), F32),
                                         nseq=BATCH, t=SEQ, row0=0, first_pos_zero=True)
        h0_8 = jnp.broadcast_to(state_h[l][:, None, :], (DEC_BATCH, SUBLANES, W_B))
        y_ab, sa_s, sb_s, hl_s = _mixers(proj, _pad_hist(state_conv_a[l], CONV_A),
                                         _pad_hist(state_conv_b[l], CONV_B), h0_8, lw, y_ab,
                                         nseq=DEC_BATCH, t=DEC_SEQ, row0=N_PROMPT,
                                         first_pos_zero=False)

        y_c = _attn_prompt(proj, sb_bias[l], u, jnp.zeros((N_TOK, W_C), F32))
        y_c = _attn_sample(proj, y_c, cache_k4, cache_v4, page_table, l, sb_bias[l], u)

        x, h2 = _merge(y_ab, y_c, proj, x, lw)

        last = l == depth - 1
        if l % 2 == 0:
            i = l // 2
            x = _ffn(h2, x, ffn_w1[i].astype(BF16), ffn_w3[i].astype(BF16), ffn_w2[i].astype(BF16),
                     norm_final, final_norm=last)
        else:
            i = l // 2
            router_pad = jnp.pad(moe_router[i], ((0, 0), (0, LANES - N_EXPERTS)))
            route, cnt = _router(x, norm_ffn[l], router_pad, u)
            pos_blk, tile_expert, n_active = _route_plan(route, cnt)
            xs3 = _dispatch(pos_blk, x.reshape(N_TOK, SUBLANES, LANES))
            ys3 = _moe_ffn(tile_expert, n_active, xs3, norm_ffn[l], moe_w1[i].astype(BF16),
                           moe_w3[i].astype(BF16), moe_w2[i].astype(BF16))
            x = _combine(pos_blk, route, x, norm_final, ys3, final_norm=last)

        kp = proj[:N_PROMPT, COL_K:COL_K + W_C].reshape(BATCH, SEQ, H_C, DH_C)
        vp = proj[:N_PROMPT, COL_V:COL_V + W_C].reshape(BATCH, SEQ, H_C, DH_C)
        ks = proj[N_PROMPT:, COL_K:COL_K + W_C].reshape(DEC_BATCH, DEC_SEQ, H_C, DH_C)
        vs = proj[N_PROMPT:, COL_V:COL_V + W_C].reshape(DEC_BATCH, DEC_SEQ, H_C, DH_C)
        st_p.append((kp, vp, sa_p[:, SUBLANES - (CONV_A - 1):], sb_p[:, SUBLANES - (CONV_B - 1):],
                     hl_p[:, 0]))
        st_s.append((ks, vs, sa_s[:, SUBLANES - (CONV_A - 1):], sb_s[:, SUBLANES - (CONV_B - 1):],
                     hl_s[:, 0]))

    y_prompt = x[:N_PROMPT].reshape(BATCH, SEQ, D_MODEL)
    y_sample = x[N_PROMPT:].reshape(DEC_BATCH, DEC_SEQ, D_MODEL)
    stack = lambda sts, k: jnp.stack([s[k] for s in sts])
    return (y_prompt, y_sample,
            stack(st_p, 0), stack(st_p, 1), stack(st_p, 2), stack(st_p, 3), stack(st_p, 4),
            stack(st_s, 0), stack(st_s, 1), stack(st_s, 2), stack(st_s, 3), stack(st_s, 4))
```

```python
import functools

import jax
import jax.numpy as jnp
from jax import lax
from jax.experimental import pallas as pl
from jax.experimental.pallas import tpu as pltpu

F32 = jnp.float32
BF16 = jnp.bfloat16

D_MODEL = 1024
BATCH = 4
SEQ = 4096
DEC_BATCH = 32
DEC_SEQ = 8
PAGE_SIZE = 128
W_A = 256
W_B = 256
H_B = 4
DH_B = 64
CONV_A = 3
CONV_B = 4
RG_C = 8.0
H_C = 8
DH_C = 64
W_C = 512
IN_WIDTH = 5632
N_EXPERTS = 8
EPS = 1e-6
LOG2E = 1.4426950408889634

N_PROMPT = BATCH * SEQ
N_SAMPLE = DEC_BATCH * DEC_SEQ
N_TOK = N_PROMPT + N_SAMPLE

COL_Q = 1024
COL_K = 1536
COL_V = 2048
COL_G = 2560

SUBLANES = 8
LANES = 128
VMEM_LIMIT = 56 * 1024 * 1024

ATT_TQ = 512
ATT_TK = 256
PAGES_PER_STEP = 16

MOE_TM = 256
MOE_T = 512
MOE_TF = 512
MOE_ROWS = -(-(2 * N_TOK + N_EXPERTS * (MOE_T - 1)) // MOE_T) * MOE_T


def _cparams(sem):
    return pltpu.CompilerParams(dimension_semantics=sem, vmem_limit_bytes=VMEM_LIMIT)


def _rmsnorm(x, g):
    y = x * lax.rsqrt(jnp.mean(x * x, axis=-1, keepdims=True) + EPS)
    return y * g


def _inproj_kernel(x_ref, g_ref, w_ref, o_ref, h_scr):
    @pl.when(pl.program_id(1) == 0)
    def _():
        h_scr[...] = _rmsnorm(x_ref[...], g_ref[...]).astype(BF16)

    o_ref[...] = jnp.dot(h_scr[...], w_ref[...], preferred_element_type=F32)


def _in_proj(x, g, w_bf16):
    tm, tn = 1280, 512
    return pl.pallas_call(
        _inproj_kernel,
        grid=(N_TOK // tm, IN_WIDTH // tn),
        in_specs=[
            pl.BlockSpec((tm, D_MODEL), lambda i, j: (i, 0)),
            pl.BlockSpec((1, D_MODEL), lambda i, j: (0, 0)),
            pl.BlockSpec((D_MODEL, tn), lambda i, j: (0, j)),
        ],
        out_specs=pl.BlockSpec((tm, tn), lambda i, j: (i, j)),
        out_shape=jax.ShapeDtypeStruct((N_TOK, IN_WIDTH), F32),
        scratch_shapes=[pltpu.VMEM((tm, D_MODEL), BF16)],
        compiler_params=_cparams(("parallel", "arbitrary")),
        name="in_proj",
    )(x, g.reshape(1, D_MODEL), w_bf16)


def _softplus(x):
    return jnp.maximum(x, 0.0) + jnp.log1p(jnp.exp(-jnp.abs(x)))


def _mixers_kernel(p_ref, ha_ref, hb_ref, h0_ref, wa_ref, wb_ref, bb_ref, wr_ref, br_ref,
                   wi_ref, bi_ref, lam_ref, y_prev_hbm, y_ref, sa_ref, sb_ref, hl_ref, ua_buf, xb_buf,
                   hc_scr, *, tt, first_pos_zero):
    del y_prev_hbm
    ti = pl.program_id(1)

    @pl.when(ti == 0)
    def _():
        ua_buf[0:SUBLANES, :] = ha_ref[...]
        xb_buf[0:SUBLANES, :] = hb_ref[...]
        hc_scr[...] = h0_ref[...]

    bg = p_ref[:, 0:W_A]
    cg = p_ref[:, W_A:2 * W_A]
    xa = p_ref[:, 2 * W_A:3 * W_A]
    xb = p_ref[:, 3 * W_A:3 * W_A + W_B]

    ua = cg * xa
    ua_buf[SUBLANES:SUBLANES + tt, :] = ua
    ya = ua_buf[pl.ds(SUBLANES - 2, tt), :] * wa_ref[0:1, :]
    ya = ya + ua_buf[pl.ds(SUBLANES - 1, tt), :] * wa_ref[1:2, :]
    ya = ya + ua * wa_ref[2:3, :]
    y_ref[:, 0:W_A] = bg * ya

    xb_buf[SUBLANES:SUBLANES + tt, :] = xb
    xc = xb_buf[pl.ds(SUBLANES - 3, tt), :] * wb_ref[0:1, :]
    xc = xc + xb_buf[pl.ds(SUBLANES - 2, tt), :] * wb_ref[1:2, :]
    xc = xc + xb_buf[pl.ds(SUBLANES - 1, tt), :] * wb_ref[2:3, :]
    xc = xc + xb * wb_ref[3:4, :]
    xc = xc + bb_ref[...]
    xcb = xc.astype(BF16)
    r = jax.nn.sigmoid(jnp.dot(xcb, wr_ref[...], preferred_element_type=F32) + br_ref[...])
    ig = jax.nn.sigmoid(jnp.dot(xcb, wi_ref[...], preferred_element_type=F32) + bi_ref[...])
    log_a = (-RG_C * r) * _softplus(-lam_ref[...])
    a = jnp.exp(log_a)
    mult = jnp.sqrt(1.0 - jnp.exp(2.0 * log_a))
    row = lax.broadcasted_iota(jnp.int32, (tt, W_B), 0)
    if first_pos_zero:
        mult = jnp.where((row == 0) & (ti == 0), 1.0, mult)
    b = mult * (ig * xc)

    s = 1
    while s < tt:
        keep = row >= s
        a_sh = pltpu.roll(a, s, 0)
        b_sh = pltpu.roll(b, s, 0)
        b = jnp.where(keep, a * b_sh + b, b)
        a = jnp.where(keep, a * a_sh, a)
        s *= 2
    h = a * hc_scr[0:1, :] + b
    y_ref[:, W_A:W_A + W_B] = h

    last8_a = ua_buf[tt:tt + SUBLANES, :]
    last8_b = xb_buf[tt:tt + SUBLANES, :]
    hlast = jnp.broadcast_to(h[tt - 1:tt, :], (SUBLANES, W_B))
    ua_buf[0:SUBLANES, :] = last8_a
    xb_buf[0:SUBLANES, :] = last8_b
    hc_scr[...] = hlast
    sa_ref[...] = last8_a
    sb_ref[...] = last8_b
    hl_ref[...] = hlast


def _mixers(proj, hist_a8, hist_b8, h0_8, lw, y_prev, *, nseq, t, row0, first_pos_zero):
    tt = min(t, 512)
    nt = t // tt
    rb0 = row0 // tt
    kern = functools.partial(_mixers_kernel, tt=tt, first_pos_zero=first_pos_zero)
    seq3 = pl.BlockSpec((None, SUBLANES, W_A), lambda s, i: (s, 0, 0))
    full = lambda shape: pl.BlockSpec(shape, lambda s, i: (0,) * len(shape))
    in_specs = [
        pl.BlockSpec((tt, 4 * W_A), lambda s, i: (rb0 + s * nt + i, 0)),
        seq3, seq3, seq3,
        full((CONV_A, W_A)), full((CONV_B, W_B)), full((1, W_B)),
        full((W_B, W_B)), full((1, W_B)), full((W_B, W_B)), full((1, W_B)), full((1, W_B)),
        pl.BlockSpec(memory_space=pl.ANY),
    ]
    args = [proj, hist_a8, hist_b8, h0_8, lw["conv_a_w"], lw["conv_b_w"], lw["conv_b_bias"],
            lw["rg_w_r"], lw["rg_b_r"], lw["rg_w_i"], lw["rg_b_i"], lw["rg_lambda"], y_prev]
    st = jax.ShapeDtypeStruct((nseq, SUBLANES, W_A), F32)
    return pl.pallas_call(
        kern,
        grid=(nseq, nt),
        in_specs=in_specs,
        out_specs=[
            pl.BlockSpec((tt, W_A + W_B), lambda s, i: (rb0 + s * nt + i, 0)),
            seq3, seq3, seq3,
        ],
        out_shape=[jax.ShapeDtypeStruct((N_TOK, W_A + W_B), F32), st, st, st],
        scratch_shapes=[
            pltpu.VMEM((SUBLANES + tt, W_A), F32),
            pltpu.VMEM((SUBLANES + tt, W_B), F32),
            pltpu.VMEM((SUBLANES, W_B), F32),
        ],
        input_output_aliases={len(args) - 1: 0},
        compiler_params=_cparams(("parallel", "arbitrary")),
        name="mixers_t%d" % t,
    )(*args)


def _log2_sigmoid_pair(z2):
    m = jnp.minimum(z2, 0.0)
    n = m - z2
    l2 = jnp.log2(1.0 + jnp.exp2(m + n))
    return m - l2, n - l2


def _sb_chunk(qm, bias, kc, vc, u, carry, mask=None):
    z2 = lax.dot_general(qm, kc, (((1,), (1,)), ((), ())), preferred_element_type=F32) + bias
    ls, lk = _log2_sigmoid_pair(z2)
    if mask is not None:
        lk = jnp.where(mask, lk, 0.0)
    after = jnp.dot(lk.astype(BF16), u, preferred_element_type=F32)
    total = after[:, 0:1] + lk[:, 0:1]
    w = jnp.exp2(ls + after + carry)
    if mask is not None:
        w = jnp.where(mask, w, 0.0)
    return jnp.dot(w.astype(BF16), vc, preferred_element_type=F32), total


def _attn_prompt_kernel(bias_ref, q_ref, k_ref, v_ref, u_ref, y_prev_hbm, o_ref, kb_scr, vb_scr,
                        acc_scr, cr_scr):
    del y_prev_hbm
    hp = pl.program_id(1)
    qi = pl.program_id(2)
    tq, tk = ATT_TQ, ATT_TK
    nblk = tq // tk

    @pl.when(qi == 0)
    def _():
        kb_scr[...] = k_ref[...].astype(BF16)
        vb_scr[...] = v_ref[...].astype(BF16)

    q = q_ref[...] * (LOG2E * DH_C ** -0.5)
    lane = lax.broadcasted_iota(jnp.int32, (tq, LANES), 1)
    qm = [jnp.where(lane < DH_C, q, 0.0).astype(BF16), jnp.where(lane >= DH_C, q, 0.0).astype(BF16)]
    bias = [bias_ref[2 * hp] * LOG2E, bias_ref[2 * hp + 1] * LOG2E]
    u = u_ref[...]
    row = lax.broadcasted_iota(jnp.int32, (tq, tk), 0)
    col = lax.broadcasted_iota(jnp.int32, (tq, tk), 1)

    def block(blk, hh, first, mask):
        st = pl.multiple_of(blk * tk, tk)
        carry = 0.0 if first else cr_scr[hh, :, 0:1]
        pv, total = _sb_chunk(qm[hh], bias[hh], kb_scr[pl.ds(st, tk), :], vb_scr[pl.ds(st, tk), :],
                              u, carry, mask)
        total = jnp.broadcast_to(total, (tq, LANES))
        if first:
            acc_scr[hh] = pv
            cr_scr[hh] = total
        else:
            acc_scr[hh] += pv
            cr_scr[hh] += total

    for d in range(nblk - 1, -1, -1):
        causal = col + d * tk < row
        for hh in range(2):
            block(qi * nblk + d, hh, d == nblk - 1, causal)

    def body(jj, _):
        for hh in range(2):
            block(qi * nblk - jj, hh, False, None)
        return 0

    lax.fori_loop(1, qi * nblk + 1, body, 0)
    o_ref[...] = jnp.where(lane < DH_C, acc_scr[0], acc_scr[1])


def _attn_prompt(proj, sb_bias, u, y_prev):
    tq = ATT_TQ
    nq = SEQ // tq
    grid_spec = pltpu.PrefetchScalarGridSpec(
        num_scalar_prefetch=1,
        grid=(BATCH, H_C // 2, nq),
        in_specs=[
            pl.BlockSpec((tq, LANES), lambda b, hp, qi, s: (b * nq + qi, COL_Q // LANES + hp)),
            pl.BlockSpec((SEQ, LANES), lambda b, hp, qi, s: (b, COL_K // LANES + hp)),
            pl.BlockSpec((SEQ, LANES), lambda b, hp, qi, s: (b, COL_V // LANES + hp)),
            pl.BlockSpec((ATT_TK, ATT_TK), lambda b, hp, qi, s: (0, 0)),
            pl.BlockSpec(memory_space=pl.ANY),
        ],
        out_specs=pl.BlockSpec((tq, LANES), lambda b, hp, qi, s: (b * nq + qi, hp)),
        scratch_shapes=[
            pltpu.VMEM((SEQ, LANES), BF16),
            pltpu.VMEM((SEQ, LANES), BF16),
            pltpu.VMEM((2, tq, LANES), F32),
            pltpu.VMEM((2, tq, LANES), F32),
        ],
    )
    return pl.pallas_call(
        _attn_prompt_kernel,
        grid_spec=grid_spec,
        out_shape=jax.ShapeDtypeStruct((N_TOK, W_C), F32),
        input_output_aliases={5: 0},
        compiler_params=_cparams(("parallel", "parallel", "arbitrary")),
        name="attn_prompt",
    )(sb_bias, proj, proj, proj, u, y_prev)


def _attn_sample_kernel(pt_ref, bias_ref, q_ref, kn_ref, vn_ref, u_ref, *rest):
    npg = PAGES_PER_STEP
    k_refs = rest[:npg]
    v_refs = rest[npg:2 * npg]
    (o_ref, qbd_scr, bias_scr, kn_scr, vn_scr, kcat_scr, vcat_scr, acc_scr,
     cr_scr) = rest[2 * npg + 1:]
    j = pl.program_id(1)
    nrow = H_C * DEC_SEQ
    u = u_ref[0:PAGE_SIZE, 0:PAGE_SIZE]

    @pl.when(j == 0)
    def _():
        row = lax.broadcasted_iota(jnp.int32, (nrow, W_C), 0)
        lane = lax.broadcasted_iota(jnp.int32, (nrow, W_C), 1)
        q8 = q_ref[...] * (LOG2E * DH_C ** -0.5)
        qt = jnp.concatenate([q8] * H_C, axis=0)
        qbd_scr[...] = jnp.where(row // DEC_SEQ == lane // DH_C, qt, 0.0).astype(BF16)
        rowb = lax.broadcasted_iota(jnp.int32, (nrow, LANES), 0)
        bias = jnp.zeros((nrow, LANES), F32)
        for h in range(H_C):
            bias = jnp.where(rowb // DEC_SEQ == h, bias_ref[h] * LOG2E, bias)
        bias_scr[...] = bias
        kn_scr[...] = jnp.zeros((PAGE_SIZE, W_C), BF16)
        vn_scr[...] = jnp.zeros((PAGE_SIZE, W_C), BF16)
        kn_scr[0:DEC_SEQ, :] = kn_ref[...].astype(BF16)
        vn_scr[0:DEC_SEQ, :] = vn_ref[...].astype(BF16)
        rowc = lax.broadcasted_iota(jnp.int32, (nrow, PAGE_SIZE), 0)
        colc = lax.broadcasted_iota(jnp.int32, (nrow, PAGE_SIZE), 1)
        pv, total = _sb_chunk(qbd_scr[...], bias_scr[:, 0:1], kn_scr[...], vn_scr[...], u, 0.0,
                              mask=colc < rowc % DEC_SEQ)
        acc_scr[...] = pv
        cr_scr[...] = jnp.broadcast_to(total, (nrow, LANES))

    for p in range(npg):
        kcat_scr[:, p * PAGE_SIZE:(p + 1) * PAGE_SIZE] = k_refs[p][...].astype(BF16)
        vcat_scr[:, p * PAGE_SIZE:(p + 1) * PAGE_SIZE] = v_refs[p][...].astype(BF16)
    z2 = jnp.dot(qbd_scr[...], kcat_scr[...], preferred_element_type=F32)
    ls, lk = _log2_sigmoid_pair(z2 + bias_scr[:, 0:1])
    lkb = lk.astype(BF16)
    carry = cr_scr[:, 0:1]
    afters = []
    for p in range(npg):
        sl = slice(p * PAGE_SIZE, (p + 1) * PAGE_SIZE)
        a = jnp.dot(lkb[:, sl], u, preferred_element_type=F32)
        afters.append(a + carry)
        carry = carry + (a[:, 0:1] + lk[:, p * PAGE_SIZE:p * PAGE_SIZE + 1])
    w = jnp.exp2(ls + jnp.concatenate(afters, axis=1)).astype(BF16)
    acc_scr[...] += lax.dot_general(w, vcat_scr[...], (((1,), (1,)), ((), ())),
                                    preferred_element_type=F32)
    cr_scr[...] = jnp.broadcast_to(carry, (nrow, LANES))

    @pl.when(j == pl.num_programs(1) - 1)
    def _():
        lane = lax.broadcasted_iota(jnp.int32, (DEC_SEQ, W_C), 1)
        out = jnp.zeros((DEC_SEQ, W_C), F32)
        for h in range(H_C):
            out = out + jnp.where(lane // DH_C == h, acc_scr[h * DEC_SEQ:(h + 1) * DEC_SEQ, :], 0.0)
        o_ref[...] = out


def _attn_sample(proj, y_c, cache_k4, cache_v4, page_table, layer, sb_bias, u):
    npg = PAGES_PER_STEP
    n_pages = page_table.shape[1]
    nsteps = n_pages // npg
    rb0 = N_PROMPT // DEC_SEQ
    nrow = H_C * DEC_SEQ

    def row_spec(colblk):
        return pl.BlockSpec((DEC_SEQ, W_C), lambda b, j, pt, s: (rb0 + b, colblk))

    def page_spec(p):
        return pl.BlockSpec((None, None, W_C, PAGE_SIZE),
                            lambda b, j, pt, s: (layer, pt[b, n_pages - 1 - (j * npg + p)], 0, 0))

    grid_spec = pltpu.PrefetchScalarGridSpec(
        num_scalar_prefetch=2,
        grid=(DEC_BATCH, nsteps),
        in_specs=([row_spec(COL_Q // W_C), row_spec(COL_K // W_C), row_spec(COL_V // W_C),
                   pl.BlockSpec((ATT_TK, ATT_TK), lambda b, j, pt, s: (0, 0))]
                  + [page_spec(p) for p in range(npg)]
                  + [page_spec(p) for p in range(npg)]
                  + [pl.BlockSpec(memory_space=pl.ANY)]),
        out_specs=pl.BlockSpec((DEC_SEQ, W_C), lambda b, j, pt, s: (rb0 + b, 0)),
        scratch_shapes=[
            pltpu.VMEM((nrow, W_C), BF16),
            pltpu.VMEM((nrow, LANES), F32),
            pltpu.VMEM((PAGE_SIZE, W_C), BF16),
            pltpu.VMEM((PAGE_SIZE, W_C), BF16),
            pltpu.VMEM((W_C, npg * PAGE_SIZE), BF16),
            pltpu.VMEM((W_C, npg * PAGE_SIZE), BF16),
            pltpu.VMEM((nrow, W_C), F32),
            pltpu.VMEM((nrow, LANES), F32),
        ],
    )
    n_in = 2 + 4 + 2 * npg
    return pl.pallas_call(
        _attn_sample_kernel,
        grid_spec=grid_spec,
        out_shape=jax.ShapeDtypeStruct((N_TOK, W_C), F32),
        input_output_aliases={n_in: 0},
        compiler_params=_cparams(("parallel", "arbitrary")),
        name="attn_sample",
    )(page_table, sb_bias, proj, proj, proj, u, *([cache_k4] * npg), *([cache_v4] * npg), y_c)


def _merge(y_ab, y_c, proj, x, lw):
    tm = 640
    rows = lambda w, c: pl.BlockSpec((tm, w), lambda i: (i, c))
    full = lambda shape: pl.BlockSpec(shape, lambda i: (0, 0))
    return pl.pallas_call(
        _merge_kernel_split,
        grid=(N_TOK // tm,),
        in_specs=[
            rows(W_A + W_B, 0), rows(W_C, 0),
            *[pl.BlockSpec((tm, 512), functools.partial(lambda i, c: (i, c), c=COL_G // 512 + k))
              for k in range(6)],
            rows(D_MODEL, 0),
            full((W_A, D_MODEL)), full((W_B, D_MODEL)), full((W_C, D_MODEL)),
            full((D_MODEL, D_MODEL)), full((1, D_MODEL)),
        ],
        out_specs=[rows(D_MODEL, 0), rows(D_MODEL, 0)],
        out_shape=[jax.ShapeDtypeStruct((N_TOK, D_MODEL), F32),
                   jax.ShapeDtypeStruct((N_TOK, D_MODEL), BF16)],
        compiler_params=_cparams(("parallel",)),
        name="merge",
    )(y_ab, y_c, proj, proj, proj, proj, proj, proj, x, lw["w_br_a"], lw["w_br_b"], lw["w_br_c"],
      lw["w_out"], lw["norm_ffn"])


def _merge_kernel_split(yab_ref, yc_ref, ga0, ga1, gb0, gb1, gc0, gc1, x_ref, wa_ref, wb_ref,
                        wc_ref, wo_ref, gn_ref, xo_ref, h2_ref):
    ya = yab_ref[:, 0:W_A].astype(BF16)
    yb = yab_ref[:, W_A:W_A + W_B].astype(BF16)
    yc = yc_ref[...].astype(BF16)
    pa = jnp.dot(ya, wa_ref[...], preferred_element_type=F32)
    pb = jnp.dot(yb, wb_ref[...], preferred_element_type=F32)
    pc = jnp.dot(yc, wc_ref[...], preferred_element_type=F32)
    halves = []
    for k, (ga, gb, gc) in enumerate(((ga0, gb0, gc0), (ga1, gb1, gc1))):
        sl = slice(k * 512, (k + 1) * 512)
        m = jax.nn.sigmoid(ga[...]) * pa[:, sl]
        m = m + jax.nn.sigmoid(gb[...]) * pb[:, sl]
        m = m + jax.nn.sigmoid(gc[...]) * pc[:, sl]
        halves.append(m.astype(BF16))
    m = jnp.concatenate(halves, axis=1)
    xn = x_ref[...] + jnp.dot(m, wo_ref[...], preferred_element_type=F32)
    xo_ref[...] = xn
    h2_ref[...] = _rmsnorm(xn, gn_ref[...]).astype(BF16)


def _router_kernel(x_ref, gn_ref, wr_ref, u_ref, route_ref, cnt_ref, cnt_scr):
    @pl.when(pl.program_id(0) == 0)
    def _():
        cnt_scr[...] = jnp.zeros_like(cnt_scr)

    h = _rmsnorm(x_ref[...], gn_ref[...])
    logits = jnp.dot(h, wr_ref[...], preferred_element_type=F32, precision=lax.Precision.HIGHEST)
    lane = lax.broadcasted_iota(jnp.int32, logits.shape, 1)
    neg = jnp.float32(-jnp.inf)
    logits = jnp.where(lane < N_EXPERTS, logits, neg)
    v1 = jnp.max(logits, axis=-1, keepdims=True)
    i1 = jnp.min(jnp.where(logits == v1, lane, LANES), axis=-1, keepdims=True)
    rest = jnp.where(lane == i1, neg, logits)
    v2 = jnp.max(rest, axis=-1, keepdims=True)
    i2 = jnp.min(jnp.where(rest == v2, lane, LANES), axis=-1, keepdims=True)
    e2 = jnp.exp(v2 - v1)
    den = 1.0 + e2
    oh1 = lane == i1
    oh2 = lane == i2
    m = jnp.where(oh1 | oh2, 1.0, 0.0)
    before = jnp.dot(u_ref[...], m.astype(BF16), preferred_element_type=F32) + cnt_scr[0:1, :]
    r1 = jnp.sum(jnp.where(oh1, before, 0.0), axis=-1, keepdims=True)
    r2 = jnp.sum(jnp.where(oh2, before, 0.0), axis=-1, keepdims=True)
    cols = (1.0 / den, e2 / den, i1.astype(F32), i2.astype(F32), r1, r2)
    route = jnp.zeros(logits.shape, F32)
    for k, c in enumerate(cols):
        route = jnp.where(lane == k, c, route)
    route_ref[...] = route
    cnt = jnp.broadcast_to(cnt_scr[0:1, :] + jnp.sum(m, axis=0, keepdims=True), cnt_scr.shape)
    cnt_scr[...] = cnt
    cnt_ref[...] = cnt


def _router(x, g, router_pad, u):
    tm = MOE_TM
    return pl.pallas_call(
        _router_kernel,
        grid=(N_TOK // tm,),
        in_specs=[
            pl.BlockSpec((tm, D_MODEL), lambda i: (i, 0)),
            pl.BlockSpec((1, D_MODEL), lambda i: (0, 0)),
            pl.BlockSpec((D_MODEL, LANES), lambda i: (0, 0)),
            pl.BlockSpec((tm, tm), lambda i: (0, 0)),
        ],
        out_specs=[pl.BlockSpec((tm, LANES), lambda i: (i, 0)),
                   pl.BlockSpec((SUBLANES, LANES), lambda i: (0, 0))],
        out_shape=[jax.ShapeDtypeStruct((N_TOK, LANES), F32),
                   jax.ShapeDtypeStruct((SUBLANES, LANES), F32)],
        scratch_shapes=[pltpu.VMEM((SUBLANES, LANES), F32)],
        compiler_params=_cparams(("arbitrary",)),
        name="router",
    )(x, g.reshape(1, D_MODEL), router_pad, u)


def _route_plan(route, cnt):
    t = MOE_T
    e12 = route[:, 2:4].astype(jnp.int32)
    rank = route[:, 4:6].astype(jnp.int32)
    counts = cnt[0, :N_EXPERTS].astype(jnp.int32)
    padded = ((counts + t - 1) // t) * t
    ends = jnp.cumsum(padded)
    pos = (ends - padded)[e12] + rank
    n_active = ends[-1] // t
    tile_start = jnp.arange(MOE_ROWS // t, dtype=jnp.int32) * t
    tile_expert = jnp.sum(tile_start[:, None] >= ends[None, :], axis=1).astype(jnp.int32)
    last_expert = jnp.sum((n_active - 1) * t >= ends).astype(jnp.int32)
    tile_expert = jnp.minimum(tile_expert, last_expert)
    pos_blk = pos.reshape(N_TOK // MOE_TM, MOE_TM, 2).transpose(0, 2, 1)
    return pos_blk, tile_expert, n_active.reshape(1).astype(jnp.int32)


def _row(ref, r):
    return ref.at[pl.ds(r, 1), :]


def _dispatch_kernel(pos_ref, x_ref, xs_in_hbm, xs_hbm, sem):
    del xs_in_hbm

    def issue(r, c):
        pltpu.make_async_copy(_row(x_ref, r), _row(xs_hbm, pos_ref[0, r]), sem).start()
        pltpu.make_async_copy(_row(x_ref, r), _row(xs_hbm, pos_ref[1, r]), sem).start()
        return c

    def drain(r, c):
        pltpu.make_async_copy(_row(x_ref, 0), _row(xs_hbm, 0), sem).wait()
        pltpu.make_async_copy(_row(x_ref, 0), _row(xs_hbm, 0), sem).wait()
        return c

    lax.fori_loop(0, MOE_TM, issue, 0, unroll=8)
    lax.fori_loop(0, MOE_TM, drain, 0, unroll=8)


def _dispatch(pos_blk, x):
    zeros = jnp.zeros((MOE_ROWS, D_MODEL), F32)
    return pl.pallas_call(
        _dispatch_kernel,
        grid=(N_TOK // MOE_TM,),
        in_specs=[
            pl.BlockSpec((None, 2, MOE_TM), lambda i: (i, 0, 0), memory_space=pltpu.SMEM),
            pl.BlockSpec((MOE_TM, D_MODEL), lambda i: (i, 0)),
            pl.BlockSpec(memory_space=pl.ANY),
        ],
        out_specs=pl.BlockSpec(memory_space=pl.ANY),
        out_shape=jax.ShapeDtypeStruct((MOE_ROWS, D_MODEL), F32),
        scratch_shapes=[pltpu.SemaphoreType.DMA(())],
        input_output_aliases={2: 0},
        compiler_params=_cparams(("arbitrary",)),
        name="moe_dispatch",
    )(pos_blk, x, zeros)


def _moe_ffn_kernel(te_ref, na_ref, xs_ref, gn_ref, w1_ref, w3_ref, w2_ref, ys_ref, hb_scr, acc_scr):
    del te_ref
    f = pl.program_id(1)
    active = pl.program_id(0) < na_ref[0]

    @pl.when(active & (f == 0))
    def _():
        hb_scr[...] = _rmsnorm(xs_ref[...], gn_ref[...]).astype(BF16)
        acc_scr[...] = jnp.zeros_like(acc_scr)

    @pl.when(active)
    def _():
        h = hb_scr[...]
        a = jnp.dot(h, w1_ref[...], preferred_element_type=F32)
        b = jnp.dot(h, w3_ref[...], preferred_element_type=F32)
        t = (a * jax.nn.sigmoid(a)) * b
        acc_scr[...] += jnp.dot(t.astype(BF16), w2_ref[...], preferred_element_type=F32)

    @pl.when(f == pl.num_programs(1) - 1)
    def _():
        ys_ref[...] = jnp.where(active, acc_scr[...], 0.0)


def _moe_ffn(tile_expert, n_active, xs, g, w1, w3, w2):
    t, tf = MOE_T, MOE_TF
    nf = w1.shape[2] // tf

    def wspec(shape, order):
        def index(r, f, te, na):
            ff = jnp.where(r < na[0], f, nf - 1)
            return (te[r], ff, 0) if order else (te[r], 0, ff)
        return pl.BlockSpec(shape, index)

    grid_spec = pltpu.PrefetchScalarGridSpec(
        num_scalar_prefetch=2,
        grid=(MOE_ROWS // t, nf),
        in_specs=[
            pl.BlockSpec((t, D_MODEL), lambda r, f, te, na: (r, 0)),
            pl.BlockSpec((1, D_MODEL), lambda r, f, te, na: (0, 0)),
            wspec((None, D_MODEL, tf), False),
            wspec((None, D_MODEL, tf), False),
            wspec((None, tf, D_MODEL), True),
        ],
        out_specs=pl.BlockSpec((t, D_MODEL), lambda r, f, te, na: (r, 0)),
        scratch_shapes=[pltpu.VMEM((t, D_MODEL), BF16), pltpu.VMEM((t, D_MODEL), F32)],
    )
    return pl.pallas_call(
        _moe_ffn_kernel,
        grid_spec=grid_spec,
        out_shape=jax.ShapeDtypeStruct((MOE_ROWS, D_MODEL), F32),
        compiler_params=_cparams(("parallel", "arbitrary")),
        name="moe_ffn",
    )(tile_expert, n_active, xs, g.reshape(1, D_MODEL), w1, w3, w2)


def _combine_kernel(pos_ref, route_ref, x_ref, gf_ref, ys_hbm, o_ref, y1_buf, y2_buf, sem, *,
                    final_norm):
    def issue(r, c):
        pltpu.make_async_copy(_row(ys_hbm, pos_ref[0, r]), _row(y1_buf, r), sem).start()
        pltpu.make_async_copy(_row(ys_hbm, pos_ref[1, r]), _row(y2_buf, r), sem).start()
        return c

    def drain(r, c):
        pltpu.make_async_copy(_row(ys_hbm, 0), _row(y1_buf, 0), sem).wait()
        pltpu.make_async_copy(_row(ys_hbm, 0), _row(y2_buf, 0), sem).wait()
        return c

    lax.fori_loop(0, MOE_TM, issue, 0, unroll=8)
    lax.fori_loop(0, MOE_TM, drain, 0, unroll=8)
    y = x_ref[...] + (route_ref[:, 0:1] * y1_buf[...] + route_ref[:, 1:2] * y2_buf[...])
    if final_norm:
        y = _rmsnorm(y, gf_ref[...])
    o_ref[...] = y


def _combine(pos_blk, route, x, g_final, ys, *, final_norm):
    tm = MOE_TM
    return pl.pallas_call(
        functools.partial(_combine_kernel, final_norm=final_norm),
        grid=(N_TOK // tm,),
        in_specs=[
            pl.BlockSpec((None, 2, tm), lambda i: (i, 0, 0), memory_space=pltpu.SMEM),
            pl.BlockSpec((tm, LANES), lambda i: (i, 0)),
            pl.BlockSpec((tm, D_MODEL), lambda i: (i, 0)),
            pl.BlockSpec((1, D_MODEL), lambda i: (0, 0)),
            pl.BlockSpec(memory_space=pl.ANY),
        ],
        out_specs=pl.BlockSpec((tm, D_MODEL), lambda i: (i, 0)),
        out_shape=jax.ShapeDtypeStruct((N_TOK, D_MODEL), F32),
        scratch_shapes=[pltpu.VMEM((tm, D_MODEL), F32), pltpu.VMEM((tm, D_MODEL), F32),
                        pltpu.SemaphoreType.DMA(())],
        compiler_params=_cparams(("arbitrary",)),
        name="moe_combine",
    )(pos_blk, route, x, g_final.reshape(1, D_MODEL), ys)


def _ffn_kernel(h_ref, x_ref, w1_ref, w3_ref, w2_ref, gf_ref, o_ref, acc_scr, *, final_norm):
    f = pl.program_id(1)

    @pl.when(f == 0)
    def _():
        acc_scr[...] = jnp.zeros_like(acc_scr)

    h = h_ref[...]
    a = jnp.dot(h, w1_ref[...], preferred_element_type=F32)
    b = jnp.dot(h, w3_ref[...], preferred_element_type=F32)
    t = (a * jax.nn.sigmoid(a)) * b
    acc_scr[...] += jnp.dot(t.astype(BF16), w2_ref[...], preferred_element_type=F32)

    @pl.when(f == pl.num_programs(1) - 1)
    def _():
        y = x_ref[...] + acc_scr[...]
        if final_norm:
            y = _rmsnorm(y, gf_ref[...])
        o_ref[...] = y


def _ffn(h2, x, w1, w3, w2, g_final, *, final_norm):
    tm, tf = 1280, 256
    d_ff = w1.shape[1]
    return pl.pallas_call(
        functools.partial(_ffn_kernel, final_norm=final_norm),
        grid=(N_TOK // tm, d_ff // tf),
        in_specs=[
            pl.BlockSpec((tm, D_MODEL), lambda i, f: (i, 0)),
            pl.BlockSpec((tm, D_MODEL), lambda i, f: (i, 0)),
            pl.BlockSpec((D_MODEL, tf), lambda i, f: (0, f)),
            pl.BlockSpec((D_MODEL, tf), lambda i, f: (0, f)),
            pl.BlockSpec((tf, D_MODEL), lambda i, f: (f, 0)),
            pl.BlockSpec((1, D_MODEL), lambda i, f: (0, 0)),
        ],
        out_specs=pl.BlockSpec((tm, D_MODEL), lambda i, f: (i, 0)),
        out_shape=jax.ShapeDtypeStruct((N_TOK, D_MODEL), F32),
        scratch_shapes=[pltpu.VMEM((tm, D_MODEL), F32)],
        compiler_params=_cparams(("parallel", "arbitrary")),
        name="ffn_dense",
    )(h2, x, w1, w3, w2, g_final.reshape(1, D_MODEL))


def _pad_hist(hist, width):
    nseq, _, c = hist.shape
    return jnp.concatenate([jnp.zeros((nseq, SUBLANES - (width - 1), c), hist.dtype), hist], axis=1)


def _block_diag(w):
    h, d, _ = w.shape
    eye = jnp.eye(h, dtype=w.dtype)
    return (eye[:, None, :, None] * w[:, :, None, :]).reshape(h * d, h * d)


def kernel(x_prompt, x_sample, cache_k, cache_v, state_conv_a, state_conv_b, state_h, page_table,
           norm_mix, w_in, conv_a_w, conv_b_w, conv_b_bias, rg_w_r, rg_b_r, rg_w_i, rg_b_i,
           rg_lambda, sb_bias, w_br_a, w_br_b, w_br_c, w_out, norm_ffn, ffn_w1, ffn_w3, ffn_w2,
           moe_router, moe_w1, moe_w3, moe_w2, norm_final):
    depth = w_in.shape[0]
    n_phys = cache_k.shape[1]
    cache_k4 = cache_k.transpose(0, 1, 3, 4, 2).reshape(depth, n_phys, W_C, PAGE_SIZE)
    cache_v4 = cache_v.transpose(0, 1, 3, 4, 2).reshape(depth, n_phys, W_C, PAGE_SIZE)
    x = jnp.concatenate([x_prompt.reshape(N_PROMPT, D_MODEL), x_sample.reshape(N_SAMPLE, D_MODEL)])
    ri = lax.broadcasted_iota(jnp.int32, (ATT_TK, ATT_TK), 0)
    ci = lax.broadcasted_iota(jnp.int32, (ATT_TK, ATT_TK), 1)
    u = (ri > ci).astype(BF16)
    zeros8 = jnp.zeros((BATCH, SUBLANES, W_A), F32)

    st_p, st_s = [], []
    for l in range(depth):
        lw = {
            "conv_a_w": conv_a_w[l], "conv_b_w": conv_b_w[l], "conv_b_bias": conv_b_bias[l][None],
            "rg_w_r": _block_diag(rg_w_r[l]).astype(BF16), "rg_b_r": rg_b_r[l].reshape(1, W_B),
            "rg_w_i": _block_diag(rg_w_i[l]).astype(BF16), "rg_b_i": rg_b_i[l].reshape(1, W_B),
            "rg_lambda": rg_lambda[l][None],
            "w_br_a": w_br_a[l].astype(BF16), "w_br_b": w_br_b[l].astype(BF16),
            "w_br_c": w_br_c[l].astype(BF16), "w_out": w_out[l].astype(BF16),
            "norm_ffn": norm_ffn[l][None],
        }
        proj = _in_proj(x, norm_mix[l], w_in[l].astype(BF16))

        y_ab, sa_p, sb_p, hl_p = _mixers(proj, zeros8, zeros8, zeros8, lw, jnp.zeros((N_TOK, W_C), F32),
                                         nseq=BATCH, t=SEQ, row0=0, first_pos_zero=True)
        h0_8 = jnp.broadcast_to(state_h[l][:, None, :], (DEC_BATCH, SUBLANES, W_B))
        y_ab, sa_s, sb_s, hl_s = _mixers(proj, _pad_hist(state_conv_a[l], CONV_A),
                                         _pad_hist(state_conv_b[l], CONV_B), h0_8, lw, y_ab,
                                         nseq=DEC_BATCH, t=DEC_SEQ, row0=N_PROMPT,
                                         first_pos_zero=False)

        y_c = _attn_prompt(proj, sb_bias[l], u, jnp.zeros((N_TOK, W_C), F32))
        y_c = _attn_sample(proj, y_c, cache_k4, cache_v4, page_table, l, sb_bias[l], u)

        x, h2 = _merge(y_ab, y_c, proj, x, lw)

        last = l == depth - 1
        if l % 2 == 0:
            i = l // 2
            x = _ffn(h2, x, ffn_w1[i].astype(BF16), ffn_w3[i].astype(BF16), ffn_w2[i].astype(BF16),
                     norm_final, final_norm=last)
        else:
            i = l // 2
            router_pad = jnp.pad(moe_router[i], ((0, 0), (0, LANES - N_EXPERTS)))
            route, cnt = _router(x, norm_ffn[l], router_pad, u)
            pos_blk, tile_expert, n_active = _route_plan(route, cnt)
            xs = _dispatch(pos_blk, x)
            ys = _moe_ffn(tile_expert, n_active, xs, norm_ffn[l], moe_w1[i].astype(BF16),
                          moe_w3[i].astype(BF16), moe_w2[i].astype(BF16))
            x = _combine(pos_blk, route, x, norm_final, ys, final_norm=last)

        kp = proj[:N_PROMPT, COL_K:COL_K + W_C].reshape(BATCH, SEQ, H_C, DH_C)
        vp = proj[:N_PROMPT, COL_V:COL_V + W_C].reshape(BATCH, SEQ, H_C, DH_C)
        ks = proj[N_PROMPT:, COL_K:COL_K + W_C].reshape(DEC_BATCH, DEC_SEQ, H_C, DH_C)
        vs = proj[N_PROMPT:, COL_V:COL_V + W_C].reshape(DEC_BATCH, DEC_SEQ, H_C, DH_C)
        st_p.append((kp, vp, sa_p[:, SUBLANES - (CONV_A - 1):], sb_p[:, SUBLANES - (CONV_B - 1):],
                     hl_p[:, 0]))
        st_s.append((ks, vs, sa_s[:, SUBLANES - (CONV_A - 1):], sb_s[:, SUBLANES - (CONV_B - 1):],
                     hl_s[:, 0]))

    y_prompt = x[:N_PROMPT].reshape(BATCH, SEQ, D_MODEL)
    y_sample = x[N_PROMPT:].reshape(DEC_BATCH, DEC_SEQ, D_MODEL)
    stack = lambda sts, k: jnp.stack([s[k] for s in sts])
    return (y_prompt, y_sample,
            stack(st_p, 0), stack(st_p, 1), stack(st_p, 2), stack(st_p, 3), stack(st_p, 4),
            stack(st_s, 0), stack(st_s, 1), stack(st_s, 2), stack(st_s, 3), stack(st_s, 4))
```

```python
import functools

import jax
import jax.numpy as jnp
from jax import lax
from jax.experimental import pallas as pl
from jax.experimental.pallas import tpu as pltpu

F32 = jnp.float32
BF16 = jnp.bfloat16

D_MODEL = 1024
BATCH = 4
SEQ = 4096
DEC_BATCH = 32
DEC_SEQ = 8
PAGE_SIZE = 128
W_A = 256
W_B = 256
H_B = 4
DH_B = 64
CONV_A = 3
CONV_B = 4
RG_C = 8.0
H_C = 8
DH_C = 64
W_C = 512
IN_WIDTH = 5632
N_EXPERTS = 8
EPS = 1e-6
LOG2E = 1.4426950408889634

N_PROMPT = BATCH * SEQ
N_SAMPLE = DEC_BATCH * DEC_SEQ
N_TOK = N_PROMPT + N_SAMPLE

COL_Q = 1024
COL_K = 1536
COL_V = 2048
COL_G = 2560

SUBLANES = 8
LANES = 128
VMEM_LIMIT = 56 * 1024 * 1024

ATT_TQ = 512
ATT_TK = 256
PAGES_PER_STEP = 16

MOE_TM = 256
MOE_T = 512
MOE_TF = 896
MOE_ROWS = -(-(2 * N_TOK + N_EXPERTS * (MOE_T - 1)) // MOE_T) * MOE_T


def _cparams(sem):
    return pltpu.CompilerParams(dimension_semantics=sem, vmem_limit_bytes=VMEM_LIMIT)


def _rmsnorm(x, g):
    y = x * lax.rsqrt(jnp.mean(x * x, axis=-1, keepdims=True) + EPS)
    return y * g


def _inproj_kernel(x_ref, g_ref, w_ref, o_ref, h_scr):
    @pl.when(pl.program_id(1) == 0)
    def _():
        h_scr[...] = _rmsnorm(x_ref[...], g_ref[...]).astype(BF16)

    o_ref[...] = jnp.dot(h_scr[...], w_ref[...], preferred_element_type=F32)


def _in_proj(x, g, w_bf16):
    tm, tn = 1280, 1408
    return pl.pallas_call(
        _inproj_kernel,
        grid=(N_TOK // tm, IN_WIDTH // tn),
        in_specs=[
            pl.BlockSpec((tm, D_MODEL), lambda i, j: (i, 0)),
            pl.BlockSpec((1, D_MODEL), lambda i, j: (0, 0)),
            pl.BlockSpec((D_MODEL, tn), lambda i, j: (0, j)),
        ],
        out_specs=pl.BlockSpec((tm, tn), lambda i, j: (i, j)),
        out_shape=jax.ShapeDtypeStruct((N_TOK, IN_WIDTH), F32),
        scratch_shapes=[pltpu.VMEM((tm, D_MODEL), BF16)],
        compiler_params=_cparams(("parallel", "arbitrary")),
        name="in_proj",
    )(x, g.reshape(1, D_MODEL), w_bf16)


def _softplus(x):
    return jnp.maximum(x, 0.0) + jnp.log1p(jnp.exp(-jnp.abs(x)))


def _mixers_kernel(p_ref, ha_ref, hb_ref, h0_ref, wa_ref, wb_ref, bb_ref, wr_ref, br_ref,
                   wi_ref, bi_ref, lam_ref, y_prev_hbm, y_ref, sa_ref, sb_ref, hl_ref, ua_buf, xb_buf,
                   hc_scr, *, tt, first_pos_zero):
    del y_prev_hbm
    ti = pl.program_id(1)

    @pl.when(ti == 0)
    def _():
        ua_buf[0:SUBLANES, :] = ha_ref[...]
        xb_buf[0:SUBLANES, :] = hb_ref[...]
        hc_scr[...] = h0_ref[...]

    bg = p_ref[:, 0:W_A]
    cg = p_ref[:, W_A:2 * W_A]
    xa = p_ref[:, 2 * W_A:3 * W_A]
    xb = p_ref[:, 3 * W_A:3 * W_A + W_B]

    ua = cg * xa
    ua_buf[SUBLANES:SUBLANES + tt, :] = ua
    ya = ua_buf[pl.ds(SUBLANES - 2, tt), :] * wa_ref[0:1, :]
    ya = ya + ua_buf[pl.ds(SUBLANES - 1, tt), :] * wa_ref[1:2, :]
    ya = ya + ua * wa_ref[2:3, :]
    y_ref[:, 0:W_A] = bg * ya

    xb_buf[SUBLANES:SUBLANES + tt, :] = xb
    xc = xb_buf[pl.ds(SUBLANES - 3, tt), :] * wb_ref[0:1, :]
    xc = xc + xb_buf[pl.ds(SUBLANES - 2, tt), :] * wb_ref[1:2, :]
    xc = xc + xb_buf[pl.ds(SUBLANES - 1, tt), :] * wb_ref[2:3, :]
    xc = xc + xb * wb_ref[3:4, :]
    xc = xc + bb_ref[...]
    xcb = xc.astype(BF16)
    r = jax.nn.sigmoid(jnp.dot(xcb, wr_ref[...], preferred_element_type=F32) + br_ref[...])
    ig = jax.nn.sigmoid(jnp.dot(xcb, wi_ref[...], preferred_element_type=F32) + bi_ref[...])
    log_a = (-RG_C * r) * _softplus(-lam_ref[...])
    a = jnp.exp(log_a)
    mult = jnp.sqrt(1.0 - jnp.exp(2.0 * log_a))
    row = lax.broadcasted_iota(jnp.int32, (tt, W_B), 0)
    if first_pos_zero:
        mult = jnp.where((row == 0) & (ti == 0), 1.0, mult)
    b = mult * (ig * xc)

    s = 1
    while s < tt:
        keep = row >= s
        a_sh = pltpu.roll(a, s, 0)
        b_sh = pltpu.roll(b, s, 0)
        b = jnp.where(keep, a * b_sh + b, b)
        a = jnp.where(keep, a * a_sh, a)
        s *= 2
    h = a * hc_scr[0:1, :] + b
    y_ref[:, W_A:W_A + W_B] = h

    last8_a = ua_buf[tt:tt + SUBLANES, :]
    last8_b = xb_buf[tt:tt + SUBLANES, :]
    hlast = jnp.broadcast_to(h[tt - 1:tt, :], (SUBLANES, W_B))
    ua_buf[0:SUBLANES, :] = last8_a
    xb_buf[0:SUBLANES, :] = last8_b
    hc_scr[...] = hlast
    sa_ref[...] = last8_a
    sb_ref[...] = last8_b
    hl_ref[...] = hlast


def _mixers(proj, hist_a8, hist_b8, h0_8, lw, y_prev, *, nseq, t, row0, first_pos_zero):
    tt = min(t, 512)
    nt = t // tt
    rb0 = row0 // tt
    kern = functools.partial(_mixers_kernel, tt=tt, first_pos_zero=first_pos_zero)
    seq3 = pl.BlockSpec((None, SUBLANES, W_A), lambda s, i: (s, 0, 0))
    full = lambda shape: pl.BlockSpec(shape, lambda s, i: (0,) * len(shape))
    in_specs = [
        pl.BlockSpec((tt, 4 * W_A), lambda s, i: (rb0 + s * nt + i, 0)),
        seq3, seq3, seq3,
        full((CONV_A, W_A)), full((CONV_B, W_B)), full((1, W_B)),
        full((W_B, W_B)), full((1, W_B)), full((W_B, W_B)), full((1, W_B)), full((1, W_B)),
        pl.BlockSpec(memory_space=pl.ANY),
    ]
    args = [proj, hist_a8, hist_b8, h0_8, lw["conv_a_w"], lw["conv_b_w"], lw["conv_b_bias"],
            lw["rg_w_r"], lw["rg_b_r"], lw["rg_w_i"], lw["rg_b_i"], lw["rg_lambda"], y_prev]
    st = jax.ShapeDtypeStruct((nseq, SUBLANES, W_A), F32)
    return pl.pallas_call(
        kern,
        grid=(nseq, nt),
        in_specs=in_specs,
        out_specs=[
            pl.BlockSpec((tt, W_A + W_B), lambda s, i: (rb0 + s * nt + i, 0)),
            seq3, seq3, seq3,
        ],
        out_shape=[jax.ShapeDtypeStruct((N_TOK, W_A + W_B), F32), st, st, st],
        scratch_shapes=[
            pltpu.VMEM((SUBLANES + tt, W_A), F32),
            pltpu.VMEM((SUBLANES + tt, W_B), F32),
            pltpu.VMEM((SUBLANES, W_B), F32),
        ],
        input_output_aliases={len(args) - 1: 0},
        compiler_params=_cparams(("parallel", "arbitrary")),
        name="mixers_t%d" % t,
    )(*args)


def _log2_sigmoid_pair(z2):
    m = jnp.minimum(z2, 0.0)
    n = m - z2
    l2 = jnp.log2(1.0 + jnp.exp2(m + n))
    return m - l2, n - l2


def _sb_chunk(qm, bias, kc, vc, u, carry, mask=None):
    z2 = lax.dot_general(qm, kc, (((1,), (1,)), ((), ())), preferred_element_type=F32) + bias
    ls, lk = _log2_sigmoid_pair(z2)
    if mask is not None:
        lk = jnp.where(mask, lk, 0.0)
    after = jnp.dot(lk.astype(BF16), u, preferred_element_type=F32)
    total = after[:, 0:1] + lk[:, 0:1]
    w = jnp.exp2(ls + after + carry)
    if mask is not None:
        w = jnp.where(mask, w, 0.0)
    return jnp.dot(w.astype(BF16), vc, preferred_element_type=F32), total


def _attn_prompt_kernel(bias_ref, q_ref, k_ref, v_ref, u_ref, y_prev_hbm, o_ref, kt_ref, vt_ref,
                        kb_scr, vb_scr, acc_scr, cr_scr):
    del y_prev_hbm
    hp = pl.program_id(1)
    qi = pl.program_id(2)
    tq, tk = ATT_TQ, ATT_TK
    nblk = tq // tk

    @pl.when(qi == 0)
    def _():
        kb_scr[...] = k_ref[...].astype(BF16)
        vb_scr[...] = v_ref[...].astype(BF16)
        for c in range(SEQ // tq):
            rows = slice(c * tq, (c + 1) * tq)
            kt_ref[:, rows] = k_ref[rows, :].T
            vt_ref[:, rows] = v_ref[rows, :].T

    q = q_ref[...] * (LOG2E * DH_C ** -0.5)
    lane = lax.broadcasted_iota(jnp.int32, (tq, LANES), 1)
    qm = [jnp.where(lane < DH_C, q, 0.0).astype(BF16), jnp.where(lane >= DH_C, q, 0.0).astype(BF16)]
    bias = [bias_ref[2 * hp] * LOG2E, bias_ref[2 * hp + 1] * LOG2E]
    u = u_ref[...]
    row = lax.broadcasted_iota(jnp.int32, (tq, tk), 0)
    col = lax.broadcasted_iota(jnp.int32, (tq, tk), 1)

    def block(blk, hh, first, mask):
        st = pl.multiple_of(blk * tk, tk)
        carry = 0.0 if first else cr_scr[hh, :, 0:1]
        pv, total = _sb_chunk(qm[hh], bias[hh], kb_scr[pl.ds(st, tk), :], vb_scr[pl.ds(st, tk), :],
                              u, carry, mask)
        total = jnp.broadcast_to(total, (tq, LANES))
        if first:
            acc_scr[hh] = pv
            cr_scr[hh] = total
        else:
            acc_scr[hh] += pv
            cr_scr[hh] += total

    for d in range(nblk - 1, -1, -1):
        causal = col + d * tk < row
        for hh in range(2):
            block(qi * nblk + d, hh, d == nblk - 1, causal)

    def body(jj, _):
        for hh in range(2):
            block(qi * nblk - jj, hh, False, None)
        return 0

    lax.fori_loop(1, qi * nblk + 1, body, 0)
    o_ref[...] = jnp.where(lane < DH_C, acc_scr[0], acc_scr[1])


def _attn_prompt(proj, sb_bias, u, y_prev):
    tq = ATT_TQ
    nq = SEQ // tq
    grid_spec = pltpu.PrefetchScalarGridSpec(
        num_scalar_prefetch=1,
        grid=(BATCH, H_C // 2, nq),
        in_specs=[
            pl.BlockSpec((tq, LANES), lambda b, hp, qi, s: (b * nq + qi, COL_Q // LANES + hp)),
            pl.BlockSpec((SEQ, LANES), lambda b, hp, qi, s: (b, COL_K // LANES + hp)),
            pl.BlockSpec((SEQ, LANES), lambda b, hp, qi, s: (b, COL_V // LANES + hp)),
            pl.BlockSpec((ATT_TK, ATT_TK), lambda b, hp, qi, s: (0, 0)),
            pl.BlockSpec(memory_space=pl.ANY),
        ],
        out_specs=[
            pl.BlockSpec((tq, LANES), lambda b, hp, qi, s: (b * nq + qi, hp)),
            pl.BlockSpec((None, LANES, SEQ), lambda b, hp, qi, s: (b, hp, 0)),
            pl.BlockSpec((None, LANES, SEQ), lambda b, hp, qi, s: (b, hp, 0)),
        ],
        scratch_shapes=[
            pltpu.VMEM((SEQ, LANES), BF16),
            pltpu.VMEM((SEQ, LANES), BF16),
            pltpu.VMEM((2, tq, LANES), F32),
            pltpu.VMEM((2, tq, LANES), F32),
        ],
    )
    kv_t = jax.ShapeDtypeStruct((BATCH, W_C, SEQ), F32)
    return pl.pallas_call(
        _attn_prompt_kernel,
        grid_spec=grid_spec,
        out_shape=[jax.ShapeDtypeStruct((N_TOK, W_C), F32), kv_t, kv_t],
        input_output_aliases={5: 0},
        compiler_params=_cparams(("parallel", "parallel", "arbitrary")),
        name="attn_prompt",
    )(sb_bias, proj, proj, proj, u, y_prev)


def _attn_sample_kernel(pt_ref, bias_ref, q_ref, kn_ref, vn_ref, u_ref, *rest):
    npg = PAGES_PER_STEP
    k_refs = rest[:npg]
    v_refs = rest[npg:2 * npg]
    (o_ref, qbd_scr, bias_scr, kn_scr, vn_scr, kcat_scr, vcat_scr, acc_scr,
     cr_scr) = rest[2 * npg + 1:]
    j = pl.program_id(1)
    nrow = H_C * DEC_SEQ
    u = u_ref[0:PAGE_SIZE, 0:PAGE_SIZE]

    @pl.when(j == 0)
    def _():
        row = lax.broadcasted_iota(jnp.int32, (nrow, W_C), 0)
        lane = lax.broadcasted_iota(jnp.int32, (nrow, W_C), 1)
        q8 = q_ref[...] * (LOG2E * DH_C ** -0.5)
        qt = jnp.concatenate([q8] * H_C, axis=0)
        qbd_scr[...] = jnp.where(row // DEC_SEQ == lane // DH_C, qt, 0.0).astype(BF16)
        rowb = lax.broadcasted_iota(jnp.int32, (nrow, LANES), 0)
        bias = jnp.zeros((nrow, LANES), F32)
        for h in range(H_C):
            bias = jnp.where(rowb // DEC_SEQ == h, bias_ref[h] * LOG2E, bias)
        bias_scr[...] = bias
        kn_scr[...] = jnp.zeros((PAGE_SIZE, W_C), BF16)
        vn_scr[...] = jnp.zeros((PAGE_SIZE, W_C), BF16)
        kn_scr[0:DEC_SEQ, :] = kn_ref[...].astype(BF16)
        vn_scr[0:DEC_SEQ, :] = vn_ref[...].astype(BF16)
        rowc = lax.broadcasted_iota(jnp.int32, (nrow, PAGE_SIZE), 0)
        colc = lax.broadcasted_iota(jnp.int32, (nrow, PAGE_SIZE), 1)
        pv, total = _sb_chunk(qbd_scr[...], bias_scr[:, 0:1], kn_scr[...], vn_scr[...], u, 0.0,
                              mask=colc < rowc % DEC_SEQ)
        acc_scr[...] = pv
        cr_scr[...] = jnp.broadcast_to(total, (nrow, LANES))

    for p in range(npg):
        kcat_scr[:, p * PAGE_SIZE:(p + 1) * PAGE_SIZE] = k_refs[p][...].astype(BF16)
        vcat_scr[:, p * PAGE_SIZE:(p + 1) * PAGE_SIZE] = v_refs[p][...].astype(BF16)
    z2 = jnp.dot(qbd_scr[...], kcat_scr[...], preferred_element_type=F32)
    ls, lk = _log2_sigmoid_pair(z2 + bias_scr[:, 0:1])
    lkb = lk.astype(BF16)
    carry = cr_scr[:, 0:1]
    afters = []
    for p in range(npg):
        sl = slice(p * PAGE_SIZE, (p + 1) * PAGE_SIZE)
        a = jnp.dot(lkb[:, sl], u, preferred_element_type=F32)
        afters.append(a + carry)
        carry = carry + (a[:, 0:1] + lk[:, p * PAGE_SIZE:p * PAGE_SIZE + 1])
    w = jnp.exp2(ls + jnp.concatenate(afters, axis=1)).astype(BF16)
    acc_scr[...] += lax.dot_general(w, vcat_scr[...], (((1,), (1,)), ((), ())),
                                    preferred_element_type=F32)
    cr_scr[...] = jnp.broadcast_to(carry, (nrow, LANES))

    @pl.when(j == pl.num_programs(1) - 1)
    def _():
        lane = lax.broadcasted_iota(jnp.int32, (DEC_SEQ, W_C), 1)
        out = jnp.zeros((DEC_SEQ, W_C), F32)
        for h in range(H_C):
            out = out + jnp.where(lane // DH_C == h, acc_scr[h * DEC_SEQ:(h + 1) * DEC_SEQ, :], 0.0)
        o_ref[...] = out


def _attn_sample(proj, y_c, cache_k4, cache_v4, page_table, layer, sb_bias, u):
    npg = PAGES_PER_STEP
    n_pages = page_table.shape[1]
    nsteps = n_pages // npg
    rb0 = N_PROMPT // DEC_SEQ
    nrow = H_C * DEC_SEQ

    def row_spec(colblk):
        return pl.BlockSpec((DEC_SEQ, W_C), lambda b, j, pt, s: (rb0 + b, colblk))

    def page_spec(p):
        return pl.BlockSpec((None, None, W_C, PAGE_SIZE),
                            lambda b, j, pt, s: (layer, pt[b, n_pages - 1 - (j * npg + p)], 0, 0))

    grid_spec = pltpu.PrefetchScalarGridSpec(
        num_scalar_prefetch=2,
        grid=(DEC_BATCH, nsteps),
        in_specs=([row_spec(COL_Q // W_C), row_spec(COL_K // W_C), row_spec(COL_V // W_C),
                   pl.BlockSpec((ATT_TK, ATT_TK), lambda b, j, pt, s: (0, 0))]
                  + [page_spec(p) for p in range(npg)]
                  + [page_spec(p) for p in range(npg)]
                  + [pl.BlockSpec(memory_space=pl.ANY)]),
        out_specs=pl.BlockSpec((DEC_SEQ, W_C), lambda b, j, pt, s: (rb0 + b, 0)),
        scratch_shapes=[
            pltpu.VMEM((nrow, W_C), BF16),
            pltpu.VMEM((nrow, LANES), F32),
            pltpu.VMEM((PAGE_SIZE, W_C), BF16),
            pltpu.VMEM((PAGE_SIZE, W_C), BF16),
            pltpu.VMEM((W_C, npg * PAGE_SIZE), BF16),
            pltpu.VMEM((W_C, npg * PAGE_SIZE), BF16),
            pltpu.VMEM((nrow, W_C), F32),
            pltpu.VMEM((nrow, LANES), F32),
        ],
    )
    n_in = 2 + 4 + 2 * npg
    return pl.pallas_call(
        _attn_sample_kernel,
        grid_spec=grid_spec,
        out_shape=jax.ShapeDtypeStruct((N_TOK, W_C), F32),
        input_output_aliases={n_in: 0},
        compiler_params=_cparams(("parallel", "arbitrary")),
        name="attn_sample",
    )(page_table, sb_bias, proj, proj, proj, u, *([cache_k4] * npg), *([cache_v4] * npg), y_c)


def _merge(y_ab, y_c, proj, x, lw):
    tm = 640
    rows = lambda w, c: pl.BlockSpec((tm, w), lambda i: (i, c))
    full = lambda shape: pl.BlockSpec(shape, lambda i: (0, 0))
    return pl.pallas_call(
        _merge_kernel_split,
        grid=(N_TOK // tm,),
        in_specs=[
            rows(W_A + W_B, 0), rows(W_C, 0),
            *[pl.BlockSpec((tm, 512), functools.partial(lambda i, c: (i, c), c=COL_G // 512 + k))
              for k in range(6)],
            rows(D_MODEL, 0),
            full((W_A, D_MODEL)), full((W_B, D_MODEL)), full((W_C, D_MODEL)),
            full((D_MODEL, D_MODEL)), full((1, D_MODEL)),
        ],
        out_specs=[rows(D_MODEL, 0), rows(D_MODEL, 0)],
        out_shape=[jax.ShapeDtypeStruct((N_TOK, D_MODEL), F32),
                   jax.ShapeDtypeStruct((N_TOK, D_MODEL), BF16)],
        compiler_params=_cparams(("parallel",)),
        name="merge",
    )(y_ab, y_c, proj, proj, proj, proj, proj, proj, x, lw["w_br_a"], lw["w_br_b"], lw["w_br_c"],
      lw["w_out"], lw["norm_ffn"])


def _merge_kernel_split(yab_ref, yc_ref, ga0, ga1, gb0, gb1, gc0, gc1, x_ref, wa_ref, wb_ref,
                        wc_ref, wo_ref, gn_ref, xo_ref, h2_ref):
    ya = yab_ref[:, 0:W_A].astype(BF16)
    yb = yab_ref[:, W_A:W_A + W_B].astype(BF16)
    yc = yc_ref[...].astype(BF16)
    pa = jnp.dot(ya, wa_ref[...], preferred_element_type=F32)
    pb = jnp.dot(yb, wb_ref[...], preferred_element_type=F32)
    pc = jnp.dot(yc, wc_ref[...], preferred_element_type=F32)
    halves = []
    for k, (ga, gb, gc) in enumerate(((ga0, gb0, gc0), (ga1, gb1, gc1))):
        sl = slice(k * 512, (k + 1) * 512)
        m = jax.nn.sigmoid(ga[...]) * pa[:, sl]
        m = m + jax.nn.sigmoid(gb[...]) * pb[:, sl]
        m = m + jax.nn.sigmoid(gc[...]) * pc[:, sl]
        halves.append(m.astype(BF16))
    m = jnp.concatenate(halves, axis=1)
    xn = x_ref[...] + jnp.dot(m, wo_ref[...], preferred_element_type=F32)
    xo_ref[...] = xn
    h2_ref[...] = _rmsnorm(xn, gn_ref[...]).astype(BF16)


def _router_kernel(x_ref, gn_ref, wr_ref, u_ref, route_ref, cnt_ref, cnt_scr):
    @pl.when(pl.program_id(0) == 0)
    def _():
        cnt_scr[...] = jnp.zeros_like(cnt_scr)

    h = _rmsnorm(x_ref[...], gn_ref[...])
    logits = jnp.dot(h, wr_ref[...], preferred_element_type=F32, precision=lax.Precision.HIGHEST)
    lane = lax.broadcasted_iota(jnp.int32, logits.shape, 1)
    neg = jnp.float32(-jnp.inf)
    logits = jnp.where(lane < N_EXPERTS, logits, neg)
    v1 = jnp.max(logits, axis=-1, keepdims=True)
    i1 = jnp.min(jnp.where(logits == v1, lane, LANES), axis=-1, keepdims=True)
    rest = jnp.where(lane == i1, neg, logits)
    v2 = jnp.max(rest, axis=-1, keepdims=True)
    i2 = jnp.min(jnp.where(rest == v2, lane, LANES), axis=-1, keepdims=True)
    e2 = jnp.exp(v2 - v1)
    den = 1.0 + e2
    oh1 = lane == i1
    oh2 = lane == i2
    m = jnp.where(oh1 | oh2, 1.0, 0.0)
    before = jnp.dot(u_ref[...], m.astype(BF16), preferred_element_type=F32) + cnt_scr[0:1, :]
    r1 = jnp.sum(jnp.where(oh1, before, 0.0), axis=-1, keepdims=True)
    r2 = jnp.sum(jnp.where(oh2, before, 0.0), axis=-1, keepdims=True)
    cols = (1.0 / den, e2 / den, i1.astype(F32), i2.astype(F32), r1, r2)
    route = jnp.zeros(logits.shape, F32)
    for k, c in enumerate(cols):
        route = jnp.where(lane == k, c, route)
    route_ref[...] = route
    cnt = jnp.broadcast_to(cnt_scr[0:1, :] + jnp.sum(m, axis=0, keepdims=True), cnt_scr.shape)
    cnt_scr[...] = cnt
    cnt_ref[...] = cnt


def _router(x, g, router_pad, u):
    tm = MOE_TM
    return pl.pallas_call(
        _router_kernel,
        grid=(N_TOK // tm,),
        in_specs=[
            pl.BlockSpec((tm, D_MODEL), lambda i: (i, 0)),
            pl.BlockSpec((1, D_MODEL), lambda i: (0, 0)),
            pl.BlockSpec((D_MODEL, LANES), lambda i: (0, 0)),
            pl.BlockSpec((tm, tm), lambda i: (0, 0)),
        ],
        out_specs=[pl.BlockSpec((tm, LANES), lambda i: (i, 0)),
                   pl.BlockSpec((SUBLANES, LANES), lambda i: (0, 0))],
        out_shape=[jax.ShapeDtypeStruct((N_TOK, LANES), F32),
                   jax.ShapeDtypeStruct((SUBLANES, LANES), F32)],
        scratch_shapes=[pltpu.VMEM((SUBLANES, LANES), F32)],
        compiler_params=_cparams(("arbitrary",)),
        name="router",
    )(x, g.reshape(1, D_MODEL), router_pad, u)


def _route_plan(route, cnt):
    t = MOE_T
    e12 = route[:, 2:4].astype(jnp.int32)
    rank = route[:, 4:6].astype(jnp.int32)
    counts = cnt[0, :N_EXPERTS].astype(jnp.int32)
    padded = ((counts + t - 1) // t) * t
    ends = jnp.cumsum(padded)
    pos = (ends - padded)[e12] + rank
    n_active = ends[-1] // t
    tile_start = jnp.arange(MOE_ROWS // t, dtype=jnp.int32) * t
    tile_expert = jnp.sum(tile_start[:, None] >= ends[None, :], axis=1).astype(jnp.int32)
    last_expert = jnp.sum((n_active - 1) * t >= ends).astype(jnp.int32)
    tile_expert = jnp.minimum(tile_expert, last_expert)
    pos_blk = pos.reshape(N_TOK // MOE_TM, MOE_TM, 2).transpose(0, 2, 1)
    return pos_blk, tile_expert, n_active.reshape(1).astype(jnp.int32)


def _row(ref, r):
    return ref.at[pl.ds(r, 1), :]


def _dispatch_kernel(pos_ref, x_ref, xs_in_hbm, xs_hbm, sem):
    del xs_in_hbm

    def issue(r, c):
        pltpu.make_async_copy(_row(x_ref, r), _row(xs_hbm, pos_ref[0, r]), sem).start()
        pltpu.make_async_copy(_row(x_ref, r), _row(xs_hbm, pos_ref[1, r]), sem).start()
        return c

    def drain(r, c):
        pltpu.make_async_copy(_row(x_ref, 0), _row(xs_hbm, 0), sem).wait()
        pltpu.make_async_copy(_row(x_ref, 0), _row(xs_hbm, 0), sem).wait()
        return c

    lax.fori_loop(0, MOE_TM, issue, 0, unroll=8)
    lax.fori_loop(0, MOE_TM, drain, 0, unroll=8)


def _dispatch(pos_blk, x):
    zeros = jnp.zeros((MOE_ROWS, D_MODEL), F32)
    return pl.pallas_call(
        _dispatch_kernel,
        grid=(N_TOK // MOE_TM,),
        in_specs=[
            pl.BlockSpec((None, 2, MOE_TM), lambda i: (i, 0, 0), memory_space=pltpu.SMEM),
            pl.BlockSpec((MOE_TM, D_MODEL), lambda i: (i, 0)),
            pl.BlockSpec(memory_space=pl.ANY),
        ],
        out_specs=pl.BlockSpec(memory_space=pl.ANY),
        out_shape=jax.ShapeDtypeStruct((MOE_ROWS, D_MODEL), F32),
        scratch_shapes=[pltpu.SemaphoreType.DMA(())],
        input_output_aliases={2: 0},
        compiler_params=_cparams(("arbitrary",)),
        name="moe_dispatch",
    )(pos_blk, x, zeros)


def _moe_ffn_kernel(te_ref, na_ref, xs_ref, gn_ref, w1_ref, w3_ref, w2_ref, ys_ref, hb_scr, acc_scr):
    del te_ref
    f = pl.program_id(1)
    active = pl.program_id(0) < na_ref[0]

    @pl.when(active & (f == 0))
    def _():
        hb_scr[...] = _rmsnorm(xs_ref[...], gn_ref[...]).astype(BF16)
        acc_scr[...] = jnp.zeros_like(acc_scr)

    @pl.when(active)
    def _():
        h = hb_scr[...]
        a = jnp.dot(h, w1_ref[...], preferred_element_type=F32)
        b = jnp.dot(h, w3_ref[...], preferred_element_type=F32)
        t = (a * jax.nn.sigmoid(a)) * b
        acc_scr[...] += jnp.dot(t.astype(BF16), w2_ref[...], preferred_element_type=F32)

    @pl.when(f == pl.num_programs(1) - 1)
    def _():
        ys_ref[...] = jnp.where(active, acc_scr[...], 0.0)


def _moe_ffn(tile_expert, n_active, xs, g, w1, w3, w2):
    t, tf = MOE_T, MOE_TF
    nf = w1.shape[2] // tf

    def wspec(shape, order):
        def index(r, f, te, na):
            ff = jnp.where(r < na[0], f, nf - 1)
            return (te[r], ff, 0) if order else (te[r], 0, ff)
        return pl.BlockSpec(shape, index)

    grid_spec = pltpu.PrefetchScalarGridSpec(
        num_scalar_prefetch=2,
        grid=(MOE_ROWS // t, nf),
        in_specs=[
            pl.BlockSpec((t, D_MODEL), lambda r, f, te, na: (r, 0)),
            pl.BlockSpec((1, D_MODEL), lambda r, f, te, na: (0, 0)),
            wspec((None, D_MODEL, tf), False),
            wspec((None, D_MODEL, tf), False),
            wspec((None, tf, D_MODEL), True),
        ],
        out_specs=pl.BlockSpec((t, D_MODEL), lambda r, f, te, na: (r, 0)),
        scratch_shapes=[pltpu.VMEM((t, D_MODEL), BF16), pltpu.VMEM((t, D_MODEL), F32)],
    )
    return pl.pallas_call(
        _moe_ffn_kernel,
        grid_spec=grid_spec,
        out_shape=jax.ShapeDtypeStruct((MOE_ROWS, D_MODEL), F32),
        compiler_params=_cparams(("parallel", "arbitrary")),
        name="moe_ffn",
    )(tile_expert, n_active, xs, g.reshape(1, D_MODEL), w1, w3, w2)


def _combine_kernel(pos_ref, route_ref, x_ref, gf_ref, ys_hbm, o_ref, y1_buf, y2_buf, sem, *,
                    final_norm):
    def issue(r, c):
        pltpu.make_async_copy(_row(ys_hbm, pos_ref[0, r]), _row(y1_buf, r), sem).start()
        pltpu.make_async_copy(_row(ys_hbm, pos_ref[1, r]), _row(y2_buf, r), sem).start()
        return c

    def drain(r, c):
        pltpu.make_async_copy(_row(ys_hbm, 0), _row(y1_buf, 0), sem).wait()
        pltpu.make_async_copy(_row(ys_hbm, 0), _row(y2_buf, 0), sem).wait()
        return c

    lax.fori_loop(0, MOE_TM, issue, 0, unroll=8)
    lax.fori_loop(0, MOE_TM, drain, 0, unroll=8)
    y = x_ref[...] + (route_ref[:, 0:1] * y1_buf[...] + route_ref[:, 1:2] * y2_buf[...])
    if final_norm:
        y = _rmsnorm(y, gf_ref[...])
    o_ref[...] = y


def _combine(pos_blk, route, x, g_final, ys, *, final_norm):
    tm = MOE_TM
    return pl.pallas_call(
        functools.partial(_combine_kernel, final_norm=final_norm),
        grid=(N_TOK // tm,),
        in_specs=[
            pl.BlockSpec((None, 2, tm), lambda i: (i, 0, 0), memory_space=pltpu.SMEM),
            pl.BlockSpec((tm, LANES), lambda i: (i, 0)),
            pl.BlockSpec((tm, D_MODEL), lambda i: (i, 0)),
            pl.BlockSpec((1, D_MODEL), lambda i: (0, 0)),
            pl.BlockSpec(memory_space=pl.ANY),
        ],
        out_specs=pl.BlockSpec((tm, D_MODEL), lambda i: (i, 0)),
        out_shape=jax.ShapeDtypeStruct((N_TOK, D_MODEL), F32),
        scratch_shapes=[pltpu.VMEM((tm, D_MODEL), F32), pltpu.VMEM((tm, D_MODEL), F32),
                        pltpu.SemaphoreType.DMA(())],
        compiler_params=_cparams(("arbitrary",)),
        name="moe_combine",
    )(pos_blk, route, x, g_final.reshape(1, D_MODEL), ys)


def _ffn_kernel(h_ref, x_ref, w1_ref, w3_ref, w2_ref, gf_ref, o_ref, acc_scr, *, final_norm):
    f = pl.program_id(1)

    @pl.when(f == 0)
    def _():
        acc_scr[...] = jnp.zeros_like(acc_scr)

    h = h_ref[...]
    a = jnp.dot(h, w1_ref[...], preferred_element_type=F32)
    b = jnp.dot(h, w3_ref[...], preferred_element_type=F32)
    t = (a * jax.nn.sigmoid(a)) * b
    acc_scr[...] += jnp.dot(t.astype(BF16), w2_ref[...], preferred_element_type=F32)

    @pl.when(f == pl.num_programs(1) - 1)
    def _():
        y = x_ref[...] + acc_scr[...]
        if final_norm:
            y = _rmsnorm(y, gf_ref[...])
        o_ref[...] = y


def _ffn(h2, x, w1, w3, w2, g_final, *, final_norm):
    tm, tf = 1280, 256
    d_ff = w1.shape[1]
    return pl.pallas_call(
        functools.partial(_ffn_kernel, final_norm=final_norm),
        grid=(N_TOK // tm, d_ff // tf),
        in_specs=[
            pl.BlockSpec((tm, D_MODEL), lambda i, f: (i, 0)),
            pl.BlockSpec((tm, D_MODEL), lambda i, f: (i, 0)),
            pl.BlockSpec((D_MODEL, tf), lambda i, f: (0, f)),
            pl.BlockSpec((D_MODEL, tf), lambda i, f: (0, f)),
            pl.BlockSpec((tf, D_MODEL), lambda i, f: (f, 0)),
            pl.BlockSpec((1, D_MODEL), lambda i, f: (0, 0)),
        ],
        out_specs=pl.BlockSpec((tm, D_MODEL), lambda i, f: (i, 0)),
        out_shape=jax.ShapeDtypeStruct((N_TOK, D_MODEL), F32),
        scratch_shapes=[pltpu.VMEM((tm, D_MODEL), F32)],
        compiler_params=_cparams(("parallel", "arbitrary")),
        name="ffn_dense",
    )(h2, x, w1, w3, w2, g_final.reshape(1, D_MODEL))


def _pad_hist(hist, width):
    nseq, _, c = hist.shape
    return jnp.concatenate([jnp.zeros((nseq, SUBLANES - (width - 1), c), hist.dtype), hist], axis=1)


def _block_diag(w):
    h, d, _ = w.shape
    eye = jnp.eye(h, dtype=w.dtype)
    return (eye[:, None, :, None] * w[:, :, None, :]).reshape(h * d, h * d)


def kernel(x_prompt, x_sample, cache_k, cache_v, state_conv_a, state_conv_b, state_h, page_table,
           norm_mix, w_in, conv_a_w, conv_b_w, conv_b_bias, rg_w_r, rg_b_r, rg_w_i, rg_b_i,
           rg_lambda, sb_bias, w_br_a, w_br_b, w_br_c, w_out, norm_ffn, ffn_w1, ffn_w3, ffn_w2,
           moe_router, moe_w1, moe_w3, moe_w2, norm_final):
    depth = w_in.shape[0]
    n_phys = cache_k.shape[1]
    cache_k4 = cache_k.transpose(0, 1, 3, 4, 2).reshape(depth, n_phys, W_C, PAGE_SIZE)
    cache_v4 = cache_v.transpose(0, 1, 3, 4, 2).reshape(depth, n_phys, W_C, PAGE_SIZE)
    x = jnp.concatenate([x_prompt.reshape(N_PROMPT, D_MODEL), x_sample.reshape(N_SAMPLE, D_MODEL)])
    ri = lax.broadcasted_iota(jnp.int32, (ATT_TK, ATT_TK), 0)
    ci = lax.broadcasted_iota(jnp.int32, (ATT_TK, ATT_TK), 1)
    u = (ri > ci).astype(BF16)
    zeros8 = jnp.zeros((BATCH, SUBLANES, W_A), F32)

    st_p, st_s = [], []
    for l in range(depth):
        lw = {
            "conv_a_w": conv_a_w[l], "conv_b_w": conv_b_w[l], "conv_b_bias": conv_b_bias[l][None],
            "rg_w_r": _block_diag(rg_w_r[l]).astype(BF16), "rg_b_r": rg_b_r[l].reshape(1, W_B),
            "rg_w_i": _block_diag(rg_w_i[l]).astype(BF16), "rg_b_i": rg_b_i[l].reshape(1, W_B),
            "rg_lambda": rg_lambda[l][None],
            "w_br_a": w_br_a[l].astype(BF16), "w_br_b": w_br_b[l].astype(BF16),
            "w_br_c": w_br_c[l].astype(BF16), "w_out": w_out[l].astype(BF16),
            "norm_ffn": norm_ffn[l][None],
        }
        proj = _in_proj(x, norm_mix[l], w_in[l].astype(BF16))

        y_ab, sa_p, sb_p, hl_p = _mixers(proj, zeros8, zeros8, zeros8, lw, jnp.zeros((N_TOK, W_C), F32),
                                         nseq=BATCH, t=SEQ, row0=0, first_pos_zero=True)
        h0_8 = jnp.broadcast_to(state_h[l][:, None, :], (DEC_BATCH, SUBLANES, W_B))
        y_ab, sa_s, sb_s, hl_s = _mixers(proj, _pad_hist(state_conv_a[l], CONV_A),
                                         _pad_hist(state_conv_b[l], CONV_B), h0_8, lw, y_ab,
                                         nseq=DEC_BATCH, t=DEC_SEQ, row0=N_PROMPT,
                                         first_pos_zero=False)

        y_c, kt, vt = _attn_prompt(proj, sb_bias[l], u, jnp.zeros((N_TOK, W_C), F32))
        y_c = _attn_sample(proj, y_c, cache_k4, cache_v4, page_table, l, sb_bias[l], u)

        x, h2 = _merge(y_ab, y_c, proj, x, lw)

        last = l == depth - 1
        if l % 2 == 0:
            i = l // 2
            x = _ffn(h2, x, ffn_w1[i].astype(BF16), ffn_w3[i].astype(BF16), ffn_w2[i].astype(BF16),
                     norm_final, final_norm=last)
        else:
            i = l // 2
            router_pad = jnp.pad(moe_router[i], ((0, 0), (0, LANES - N_EXPERTS)))
            route, cnt = _router(x, norm_ffn[l], router_pad, u)
            pos_blk, tile_expert, n_active = _route_plan(route, cnt)
            xs = _dispatch(pos_blk, x)
            ys = _moe_ffn(tile_expert, n_active, xs, norm_ffn[l], moe_w1[i].astype(BF16),
                          moe_w3[i].astype(BF16), moe_w2[i].astype(BF16))
            x = _combine(pos_blk, route, x, norm_final, ys, final_norm=last)

        kp = kt.reshape(BATCH, H_C, DH_C, SEQ).transpose(0, 3, 1, 2)
        vp = vt.reshape(BATCH, H_C, DH_C, SEQ).transpose(0, 3, 1, 2)
        ks = proj[N_PROMPT:, COL_K:COL_K + W_C].reshape(DEC_BATCH, DEC_SEQ, H_C, DH_C)
        vs = proj[N_PROMPT:, COL_V:COL_V + W_C].reshape(DEC_BATCH, DEC_SEQ, H_C, DH_C)
        st_p.append((kp, vp, sa_p[:, SUBLANES - (CONV_A - 1):], sb_p[:, SUBLANES - (CONV_B - 1):],
                     hl_p[:, 0]))
        st_s.append((ks, vs, sa_s[:, SUBLANES - (CONV_A - 1):], sb_s[:, SUBLANES - (CONV_B - 1):],
                     hl_s[:, 0]))

    y_prompt = x[:N_PROMPT].reshape(BATCH, SEQ, D_MODEL)
    y_sample = x[N_PROMPT:].reshape(DEC_BATCH, DEC_SEQ, D_MODEL)
    stack = lambda sts, k: jnp.stack([s[k] for s in sts])
    return (y_prompt, y_sample,
            stack(st_p, 0), stack(st_p, 1), stack(st_p, 2), stack(st_p, 3), stack(st_p, 4),
            stack(st_s, 0), stack(st_s, 1), stack(st_s, 2), stack(st_s, 3), stack(st_s, 4))
```

```python
import functools

import jax
import jax.numpy as jnp
from jax import lax
from jax.experimental import pallas as pl
from jax.experimental.pallas import tpu as pltpu

F32 = jnp.float32
BF16 = jnp.bfloat16

D_MODEL = 1024
BATCH = 4
SEQ = 4096
DEC_BATCH = 32
DEC_SEQ = 8
PAGE_SIZE = 128
W_A = 256
W_B = 256
H_B = 4
DH_B = 64
CONV_A = 3
CONV_B = 4
RG_C = 8.0
H_C = 8
DH_C = 64
W_C = 512
IN_WIDTH = 5632
N_EXPERTS = 8
EPS = 1e-6
LOG2E = 1.4426950408889634

N_PROMPT = BATCH * SEQ
N_SAMPLE = DEC_BATCH * DEC_SEQ
N_TOK = N_PROMPT + N_SAMPLE

COL_Q = 1024
COL_K = 1536
COL_V = 2048
COL_G = 2560

SUBLANES = 8
LANES = 128
VMEM_LIMIT = 56 * 1024 * 1024

ATT_TQ = 512
ATT_TK = 256
PAGES_PER_STEP = 16

MOE_TM = 256
MOE_T = 512
MOE_TF = 1792
MOE_ROWS = -(-(2 * N_TOK + N_EXPERTS * (MOE_T - 1)) // MOE_T) * MOE_T


def _cparams(sem):
    return pltpu.CompilerParams(dimension_semantics=sem, vmem_limit_bytes=VMEM_LIMIT)


def _rmsnorm(x, g):
    y = x * lax.rsqrt(jnp.mean(x * x, axis=-1, keepdims=True) + EPS)
    return y * g


def _inproj_kernel(x_ref, g_ref, w_ref, o_ref, h_scr):
    @pl.when(pl.program_id(1) == 0)
    def _():
        h_scr[...] = _rmsnorm(x_ref[...], g_ref[...]).astype(BF16)

    o_ref[...] = jnp.dot(h_scr[...], w_ref[...], preferred_element_type=F32)


def _in_proj(x, g, w_bf16):
    tm, tn = 1280, 1408
    return pl.pallas_call(
        _inproj_kernel,
        grid=(N_TOK // tm, IN_WIDTH // tn),
        in_specs=[
            pl.BlockSpec((tm, D_MODEL), lambda i, j: (i, 0)),
            pl.BlockSpec((1, D_MODEL), lambda i, j: (0, 0)),
            pl.BlockSpec((D_MODEL, tn), lambda i, j: (0, j)),
        ],
        out_specs=pl.BlockSpec((tm, tn), lambda i, j: (i, j)),
        out_shape=jax.ShapeDtypeStruct((N_TOK, IN_WIDTH), F32),
        scratch_shapes=[pltpu.VMEM((tm, D_MODEL), BF16)],
        compiler_params=_cparams(("parallel", "arbitrary")),
        name="in_proj",
    )(x, g.reshape(1, D_MODEL), w_bf16)


def _softplus(x):
    return jnp.maximum(x, 0.0) + jnp.log1p(jnp.exp(-jnp.abs(x)))


def _mixers_kernel(p_ref, ha_ref, hb_ref, h0_ref, wa_ref, wb_ref, bb_ref, wr_ref, br_ref,
                   wi_ref, bi_ref, lam_ref, y_prev_hbm, y_ref, sa_ref, sb_ref, hl_ref, ua_buf, xb_buf,
                   hc_scr, *, tt, first_pos_zero):
    del y_prev_hbm
    ti = pl.program_id(1)

    @pl.when(ti == 0)
    def _():
        ua_buf[0:SUBLANES, :] = ha_ref[...]
        xb_buf[0:SUBLANES, :] = hb_ref[...]
        hc_scr[...] = h0_ref[...]

    bg = p_ref[:, 0:W_A]
    cg = p_ref[:, W_A:2 * W_A]
    xa = p_ref[:, 2 * W_A:3 * W_A]
    xb = p_ref[:, 3 * W_A:3 * W_A + W_B]

    ua = cg * xa
    ua_buf[SUBLANES:SUBLANES + tt, :] = ua
    ya = ua_buf[pl.ds(SUBLANES - 2, tt), :] * wa_ref[0:1, :]
    ya = ya + ua_buf[pl.ds(SUBLANES - 1, tt), :] * wa_ref[1:2, :]
    ya = ya + ua * wa_ref[2:3, :]
    y_ref[:, 0:W_A] = bg * ya

    xb_buf[SUBLANES:SUBLANES + tt, :] = xb
    xc = xb_buf[pl.ds(SUBLANES - 3, tt), :] * wb_ref[0:1, :]
    xc = xc + xb_buf[pl.ds(SUBLANES - 2, tt), :] * wb_ref[1:2, :]
    xc = xc + xb_buf[pl.ds(SUBLANES - 1, tt), :] * wb_ref[2:3, :]
    xc = xc + xb * wb_ref[3:4, :]
    xc = xc + bb_ref[...]
    xcb = xc.astype(BF16)
    r = jax.nn.sigmoid(jnp.dot(xcb, wr_ref[...], preferred_element_type=F32) + br_ref[...])
    ig = jax.nn.sigmoid(jnp.dot(xcb, wi_ref[...], preferred_element_type=F32) + bi_ref[...])
    log_a = (-RG_C * r) * _softplus(-lam_ref[...])
    a = jnp.exp(log_a)
    mult = jnp.sqrt(1.0 - jnp.exp(2.0 * log_a))
    row = lax.broadcasted_iota(jnp.int32, (tt, W_B), 0)
    if first_pos_zero:
        mult = jnp.where((row == 0) & (ti == 0), 1.0, mult)
    b = mult * (ig * xc)

    s = 1
    while s < tt:
        keep = row >= s
        a_sh = pltpu.roll(a, s, 0)
        b_sh = pltpu.roll(b, s, 0)
        b = jnp.where(keep, a * b_sh + b, b)
        a = jnp.where(keep, a * a_sh, a)
        s *= 2
    h = a * hc_scr[0:1, :] + b
    y_ref[:, W_A:W_A + W_B] = h

    last8_a = ua_buf[tt:tt + SUBLANES, :]
    last8_b = xb_buf[tt:tt + SUBLANES, :]
    hlast = jnp.broadcast_to(h[tt - 1:tt, :], (SUBLANES, W_B))
    ua_buf[0:SUBLANES, :] = last8_a
    xb_buf[0:SUBLANES, :] = last8_b
    hc_scr[...] = hlast
    sa_ref[...] = last8_a
    sb_ref[...] = last8_b
    hl_ref[...] = hlast


def _mixers(proj, hist_a8, hist_b8, h0_8, lw, y_prev, *, nseq, t, row0, first_pos_zero):
    tt = min(t, 512)
    nt = t // tt
    rb0 = row0 // tt
    kern = functools.partial(_mixers_kernel, tt=tt, first_pos_zero=first_pos_zero)
    seq3 = pl.BlockSpec((None, SUBLANES, W_A), lambda s, i: (s, 0, 0))
    full = lambda shape: pl.BlockSpec(shape, lambda s, i: (0,) * len(shape))
    in_specs = [
        pl.BlockSpec((tt, 4 * W_A), lambda s, i: (rb0 + s * nt + i, 0)),
        seq3, seq3, seq3,
        full((CONV_A, W_A)), full((CONV_B, W_B)), full((1, W_B)),
        full((W_B, W_B)), full((1, W_B)), full((W_B, W_B)), full((1, W_B)), full((1, W_B)),
        pl.BlockSpec(memory_space=pl.ANY),
    ]
    args = [proj, hist_a8, hist_b8, h0_8, lw["conv_a_w"], lw["conv_b_w"], lw["conv_b_bias"],
            lw["rg_w_r"], lw["rg_b_r"], lw["rg_w_i"], lw["rg_b_i"], lw["rg_lambda"], y_prev]
    st = jax.ShapeDtypeStruct((nseq, SUBLANES, W_A), F32)
    return pl.pallas_call(
        kern,
        grid=(nseq, nt),
        in_specs=in_specs,
        out_specs=[
            pl.BlockSpec((tt, W_A + W_B), lambda s, i: (rb0 + s * nt + i, 0)),
            seq3, seq3, seq3,
        ],
        out_shape=[jax.ShapeDtypeStruct((N_TOK, W_A + W_B), F32), st, st, st],
        scratch_shapes=[
            pltpu.VMEM((SUBLANES + tt, W_A), F32),
            pltpu.VMEM((SUBLANES + tt, W_B), F32),
            pltpu.VMEM((SUBLANES, W_B), F32),
        ],
        input_output_aliases={len(args) - 1: 0},
        compiler_params=_cparams(("parallel", "arbitrary")),
        name="mixers_t%d" % t,
    )(*args)


def _log2_sigmoid_pair(z2):
    m = jnp.minimum(z2, 0.0)
    n = m - z2
    l2 = jnp.log2(1.0 + jnp.exp2(m + n))
    return m - l2, n - l2


def _sb_chunk(qm, bias, kc, vc, u, carry, mask=None):
    z2 = lax.dot_general(qm, kc, (((1,), (1,)), ((), ())), preferred_element_type=F32) + bias
    ls, lk = _log2_sigmoid_pair(z2)
    if mask is not None:
        lk = jnp.where(mask, lk, 0.0)
    after = jnp.dot(lk.astype(BF16), u, preferred_element_type=F32)
    total = after[:, 0:1] + lk[:, 0:1]
    w = jnp.exp2(ls + after + carry)
    if mask is not None:
        w = jnp.where(mask, w, 0.0)
    return jnp.dot(w.astype(BF16), vc, preferred_element_type=F32), total


def _attn_prompt_kernel(bias_ref, q_ref, k_ref, v_ref, u_ref, y_prev_hbm, o_ref, kt_ref, vt_ref,
                        kb_scr, vb_scr, acc_scr, cr_scr):
    del y_prev_hbm
    hp = pl.program_id(1)
    qi = pl.program_id(2)
    tq, tk = ATT_TQ, ATT_TK
    nblk = tq // tk

    @pl.when(qi == 0)
    def _():
        kb_scr[...] = k_ref[...].astype(BF16)
        vb_scr[...] = v_ref[...].astype(BF16)
        for c in range(SEQ // tq):
            rows = slice(c * tq, (c + 1) * tq)
            kt_ref[:, rows] = k_ref[rows, :].T
            vt_ref[:, rows] = v_ref[rows, :].T

    q = q_ref[...] * (LOG2E * DH_C ** -0.5)
    lane = lax.broadcasted_iota(jnp.int32, (tq, LANES), 1)
    qm = [jnp.where(lane < DH_C, q, 0.0).astype(BF16), jnp.where(lane >= DH_C, q, 0.0).astype(BF16)]
    bias = [bias_ref[2 * hp] * LOG2E, bias_ref[2 * hp + 1] * LOG2E]
    u = u_ref[...]
    row = lax.broadcasted_iota(jnp.int32, (tq, tk), 0)
    col = lax.broadcasted_iota(jnp.int32, (tq, tk), 1)

    def block(blk, hh, first, mask):
        st = pl.multiple_of(blk * tk, tk)
        carry = 0.0 if first else cr_scr[hh, :, 0:1]
        pv, total = _sb_chunk(qm[hh], bias[hh], kb_scr[pl.ds(st, tk), :], vb_scr[pl.ds(st, tk), :],
                              u, carry, mask)
        total = jnp.broadcast_to(total, (tq, LANES))
        if first:
            acc_scr[hh] = pv
            cr_scr[hh] = total
        else:
            acc_scr[hh] += pv
            cr_scr[hh] += total

    for d in range(nblk - 1, -1, -1):
        causal = col + d * tk < row
        for hh in range(2):
            block(qi * nblk + d, hh, d == nblk - 1, causal)

    def pair(blk, hh):
        carry = cr_scr[hh, :, 0:1]
        st1 = pl.multiple_of(blk * tk, tk)
        st0 = pl.multiple_of((blk - 1) * tk, tk)
        pv1, t1 = _sb_chunk(qm[hh], bias[hh], kb_scr[pl.ds(st1, tk), :], vb_scr[pl.ds(st1, tk), :],
                            u, carry)
        pv0, t0 = _sb_chunk(qm[hh], bias[hh], kb_scr[pl.ds(st0, tk), :], vb_scr[pl.ds(st0, tk), :],
                            u, carry + t1)
        acc_scr[hh] += pv1 + pv0
        cr_scr[hh] += jnp.broadcast_to(t1 + t0, (tq, LANES))

    def body(it, _):
        for hh in range(2):
            pair((qi - it) * nblk - 1, hh)
        return 0

    assert nblk == 2
    lax.fori_loop(0, qi, body, 0)
    o_ref[...] = jnp.where(lane < DH_C, acc_scr[0], acc_scr[1])


def _attn_prompt(proj, sb_bias, u, y_prev):
    tq = ATT_TQ
    nq = SEQ // tq
    grid_spec = pltpu.PrefetchScalarGridSpec(
        num_scalar_prefetch=1,
        grid=(BATCH, H_C // 2, nq),
        in_specs=[
            pl.BlockSpec((tq, LANES), lambda b, hp, qi, s: (b * nq + qi, COL_Q // LANES + hp)),
            pl.BlockSpec((SEQ, LANES), lambda b, hp, qi, s: (b, COL_K // LANES + hp)),
            pl.BlockSpec((SEQ, LANES), lambda b, hp, qi, s: (b, COL_V // LANES + hp)),
            pl.BlockSpec((ATT_TK, ATT_TK), lambda b, hp, qi, s: (0, 0)),
            pl.BlockSpec(memory_space=pl.ANY),
        ],
        out_specs=[
            pl.BlockSpec((tq, LANES), lambda b, hp, qi, s: (b * nq + qi, hp)),
            pl.BlockSpec((None, LANES, SEQ), lambda b, hp, qi, s: (b, hp, 0)),
            pl.BlockSpec((None, LANES, SEQ), lambda b, hp, qi, s: (b, hp, 0)),
        ],
        scratch_shapes=[
            pltpu.VMEM((SEQ, LANES), BF16),
            pltpu.VMEM((SEQ, LANES), BF16),
            pltpu.VMEM((2, tq, LANES), F32),
            pltpu.VMEM((2, tq, LANES), F32),
        ],
    )
    kv_t = jax.ShapeDtypeStruct((BATCH, W_C, SEQ), F32)
    return pl.pallas_call(
        _attn_prompt_kernel,
        grid_spec=grid_spec,
        out_shape=[jax.ShapeDtypeStruct((N_TOK, W_C), F32), kv_t, kv_t],
        input_output_aliases={5: 0},
        compiler_params=_cparams(("parallel", "parallel", "arbitrary")),
        name="attn_prompt",
    )(sb_bias, proj, proj, proj, u, y_prev)


def _attn_sample_kernel(pt_ref, bias_ref, q_ref, kn_ref, vn_ref, u_ref, *rest):
    npg = PAGES_PER_STEP
    k_refs = rest[:npg]
    v_refs = rest[npg:2 * npg]
    (o_ref, qbd_scr, bias_scr, kn_scr, vn_scr, kcat_scr, vcat_scr, acc_scr,
     cr_scr) = rest[2 * npg + 1:]
    j = pl.program_id(1)
    nrow = H_C * DEC_SEQ
    u = u_ref[0:PAGE_SIZE, 0:PAGE_SIZE]

    @pl.when(j == 0)
    def _():
        row = lax.broadcasted_iota(jnp.int32, (nrow, W_C), 0)
        lane = lax.broadcasted_iota(jnp.int32, (nrow, W_C), 1)
        q8 = q_ref[...] * (LOG2E * DH_C ** -0.5)
        qt = jnp.concatenate([q8] * H_C, axis=0)
        qbd_scr[...] = jnp.where(row // DEC_SEQ == lane // DH_C, qt, 0.0).astype(BF16)
        rowb = lax.broadcasted_iota(jnp.int32, (nrow, LANES), 0)
        bias = jnp.zeros((nrow, LANES), F32)
        for h in range(H_C):
            bias = jnp.where(rowb // DEC_SEQ == h, bias_ref[h] * LOG2E, bias)
        bias_scr[...] = bias
        kn_scr[...] = jnp.zeros((PAGE_SIZE, W_C), BF16)
        vn_scr[...] = jnp.zeros((PAGE_SIZE, W_C), BF16)
        kn_scr[0:DEC_SEQ, :] = kn_ref[...].astype(BF16)
        vn_scr[0:DEC_SEQ, :] = vn_ref[...].astype(BF16)
        rowc = lax.broadcasted_iota(jnp.int32, (nrow, PAGE_SIZE), 0)
        colc = lax.broadcasted_iota(jnp.int32, (nrow, PAGE_SIZE), 1)
        pv, total = _sb_chunk(qbd_scr[...], bias_scr[:, 0:1], kn_scr[...], vn_scr[...], u, 0.0,
                              mask=colc < rowc % DEC_SEQ)
        acc_scr[...] = pv
        cr_scr[...] = jnp.broadcast_to(total, (nrow, LANES))

    for p in range(npg):
        kcat_scr[:, p * PAGE_SIZE:(p + 1) * PAGE_SIZE] = k_refs[p][...].astype(BF16)
        vcat_scr[:, p * PAGE_SIZE:(p + 1) * PAGE_SIZE] = v_refs[p][...].astype(BF16)
    z2 = jnp.dot(qbd_scr[...], kcat_scr[...], preferred_element_type=F32)
    ls, lk = _log2_sigmoid_pair(z2 + bias_scr[:, 0:1])
    lkb = lk.astype(BF16)
    carry = cr_scr[:, 0:1]
    afters = []
    for p in range(npg):
        sl = slice(p * PAGE_SIZE, (p + 1) * PAGE_SIZE)
        a = jnp.dot(lkb[:, sl], u, preferred_element_type=F32)
        afters.append(a + carry)
        carry = carry + (a[:, 0:1] + lk[:, p * PAGE_SIZE:p * PAGE_SIZE + 1])
    w = jnp.exp2(ls + jnp.concatenate(afters, axis=1)).astype(BF16)
    acc_scr[...] += lax.dot_general(w, vcat_scr[...], (((1,), (1,)), ((), ())),
                                    preferred_element_type=F32)
    cr_scr[...] = jnp.broadcast_to(carry, (nrow, LANES))

    @pl.when(j == pl.num_programs(1) - 1)
    def _():
        lane = lax.broadcasted_iota(jnp.int32, (DEC_SEQ, W_C), 1)
        out = jnp.zeros((DEC_SEQ, W_C), F32)
        for h in range(H_C):
            out = out + jnp.where(lane // DH_C == h, acc_scr[h * DEC_SEQ:(h + 1) * DEC_SEQ, :], 0.0)
        o_ref[...] = out


def _attn_sample(proj, y_c, cache_k4, cache_v4, page_table, layer, sb_bias, u):
    npg = PAGES_PER_STEP
    n_pages = page_table.shape[1]
    nsteps = n_pages // npg
    rb0 = N_PROMPT // DEC_SEQ
    nrow = H_C * DEC_SEQ

    def row_spec(colblk):
        return pl.BlockSpec((DEC_SEQ, W_C), lambda b, j, pt, s: (rb0 + b, colblk))

    def page_spec(p):
        return pl.BlockSpec((None, None, W_C, PAGE_SIZE),
                            lambda b, j, pt, s: (layer, pt[b, n_pages - 1 - (j * npg + p)], 0, 0))

    grid_spec = pltpu.PrefetchScalarGridSpec(
        num_scalar_prefetch=2,
        grid=(DEC_BATCH, nsteps),
        in_specs=([row_spec(COL_Q // W_C), row_spec(COL_K // W_C), row_spec(COL_V // W_C),
                   pl.BlockSpec((ATT_TK, ATT_TK), lambda b, j, pt, s: (0, 0))]
                  + [page_spec(p) for p in range(npg)]
                  + [page_spec(p) for p in range(npg)]
                  + [pl.BlockSpec(memory_space=pl.ANY)]),
        out_specs=pl.BlockSpec((DEC_SEQ, W_C), lambda b, j, pt, s: (rb0 + b, 0)),
        scratch_shapes=[
            pltpu.VMEM((nrow, W_C), BF16),
            pltpu.VMEM((nrow, LANES), F32),
            pltpu.VMEM((PAGE_SIZE, W_C), BF16),
            pltpu.VMEM((PAGE_SIZE, W_C), BF16),
            pltpu.VMEM((W_C, npg * PAGE_SIZE), BF16),
            pltpu.VMEM((W_C, npg * PAGE_SIZE), BF16),
            pltpu.VMEM((nrow, W_C), F32),
            pltpu.VMEM((nrow, LANES), F32),
        ],
    )
    n_in = 2 + 4 + 2 * npg
    return pl.pallas_call(
        _attn_sample_kernel,
        grid_spec=grid_spec,
        out_shape=jax.ShapeDtypeStruct((N_TOK, W_C), F32),
        input_output_aliases={n_in: 0},
        compiler_params=_cparams(("parallel", "arbitrary")),
        name="attn_sample",
    )(page_table, sb_bias, proj, proj, proj, u, *([cache_k4] * npg), *([cache_v4] * npg), y_c)


def _merge(y_ab, y_c, proj, x, lw):
    tm = 640
    rows = lambda w, c: pl.BlockSpec((tm, w), lambda i: (i, c))
    full = lambda shape: pl.BlockSpec(shape, lambda i: (0, 0))
    return pl.pallas_call(
        _merge_kernel_split,
        grid=(N_TOK // tm,),
        in_specs=[
            rows(W_A + W_B, 0), rows(W_C, 0),
            *[pl.BlockSpec((tm, 512), functools.partial(lambda i, c: (i, c), c=COL_G // 512 + k))
              for k in range(6)],
            rows(D_MODEL, 0),
            full((W_A, D_MODEL)), full((W_B, D_MODEL)), full((W_C, D_MODEL)),
            full((D_MODEL, D_MODEL)), full((1, D_MODEL)),
        ],
        out_specs=[rows(D_MODEL, 0), rows(D_MODEL, 0)],
        out_shape=[jax.ShapeDtypeStruct((N_TOK, D_MODEL), F32),
                   jax.ShapeDtypeStruct((N_TOK, D_MODEL), BF16)],
        compiler_params=_cparams(("parallel",)),
        name="merge",
    )(y_ab, y_c, proj, proj, proj, proj, proj, proj, x, lw["w_br_a"], lw["w_br_b"], lw["w_br_c"],
      lw["w_out"], lw["norm_ffn"])


def _merge_kernel_split(yab_ref, yc_ref, ga0, ga1, gb0, gb1, gc0, gc1, x_ref, wa_ref, wb_ref,
                        wc_ref, wo_ref, gn_ref, xo_ref, h2_ref):
    ya = yab_ref[:, 0:W_A].astype(BF16)
    yb = yab_ref[:, W_A:W_A + W_B].astype(BF16)
    yc = yc_ref[...].astype(BF16)
    pa = jnp.dot(ya, wa_ref[...], preferred_element_type=F32)
    pb = jnp.dot(yb, wb_ref[...], preferred_element_type=F32)
    pc = jnp.dot(yc, wc_ref[...], preferred_element_type=F32)
    halves = []
    for k, (ga, gb, gc) in enumerate(((ga0, gb0, gc0), (ga1, gb1, gc1))):
        sl = slice(k * 512, (k + 1) * 512)
        m = jax.nn.sigmoid(ga[...]) * pa[:, sl]
        m = m + jax.nn.sigmoid(gb[...]) * pb[:, sl]
        m = m + jax.nn.sigmoid(gc[...]) * pc[:, sl]
        halves.append(m.astype(BF16))
    m = jnp.concatenate(halves, axis=1)
    xn = x_ref[...] + jnp.dot(m, wo_ref[...], preferred_element_type=F32)
    xo_ref[...] = xn
    h2_ref[...] = _rmsnorm(xn, gn_ref[...]).astype(BF16)


def _router_kernel(x_ref, gn_ref, wr_ref, u_ref, route_ref, cnt_ref, cnt_scr):
    @pl.when(pl.program_id(0) == 0)
    def _():
        cnt_scr[...] = jnp.zeros_like(cnt_scr)

    h = _rmsnorm(x_ref[...], gn_ref[...])
    logits = jnp.dot(h, wr_ref[...], preferred_element_type=F32, precision=lax.Precision.HIGHEST)
    lane = lax.broadcasted_iota(jnp.int32, logits.shape, 1)
    neg = jnp.float32(-jnp.inf)
    logits = jnp.where(lane < N_EXPERTS, logits, neg)
    v1 = jnp.max(logits, axis=-1, keepdims=True)
    i1 = jnp.min(jnp.where(logits == v1, lane, LANES), axis=-1, keepdims=True)
    rest = jnp.where(lane == i1, neg, logits)
    v2 = jnp.max(rest, axis=-1, keepdims=True)
    i2 = jnp.min(jnp.where(rest == v2, lane, LANES), axis=-1, keepdims=True)
    e2 = jnp.exp(v2 - v1)
    den = 1.0 + e2
    oh1 = lane == i1
    oh2 = lane == i2
    m = jnp.where(oh1 | oh2, 1.0, 0.0)
    before = jnp.dot(u_ref[...], m.astype(BF16), preferred_element_type=F32) + cnt_scr[0:1, :]
    r1 = jnp.sum(jnp.where(oh1, before, 0.0), axis=-1, keepdims=True)
    r2 = jnp.sum(jnp.where(oh2, before, 0.0), axis=-1, keepdims=True)
    cols = (1.0 / den, e2 / den, i1.astype(F32), i2.astype(F32), r1, r2)
    route = jnp.zeros(logits.shape, F32)
    for k, c in enumerate(cols):
        route = jnp.where(lane == k, c, route)
    route_ref[...] = route
    cnt = jnp.broadcast_to(cnt_scr[0:1, :] + jnp.sum(m, axis=0, keepdims=True), cnt_scr.shape)
    cnt_scr[...] = cnt
    cnt_ref[...] = cnt


def _router(x, g, router_pad, u):
    tm = MOE_TM
    return pl.pallas_call(
        _router_kernel,
        grid=(N_TOK // tm,),
        in_specs=[
            pl.BlockSpec((tm, D_MODEL), lambda i: (i, 0)),
            pl.BlockSpec((1, D_MODEL), lambda i: (0, 0)),
            pl.BlockSpec((D_MODEL, LANES), lambda i: (0, 0)),
            pl.BlockSpec((tm, tm), lambda i: (0, 0)),
        ],
        out_specs=[pl.BlockSpec((tm, LANES), lambda i: (i, 0)),
                   pl.BlockSpec((SUBLANES, LANES), lambda i: (0, 0))],
        out_shape=[jax.ShapeDtypeStruct((N_TOK, LANES), F32),
                   jax.ShapeDtypeStruct((SUBLANES, LANES), F32)],
        scratch_shapes=[pltpu.VMEM((SUBLANES, LANES), F32)],
        compiler_params=_cparams(("arbitrary",)),
        name="router",
    )(x, g.reshape(1, D_MODEL), router_pad, u)


def _route_plan(route, cnt):
    t = MOE_T
    e12 = route[:, 2:4].astype(jnp.int32)
    rank = route[:, 4:6].astype(jnp.int32)
    counts = cnt[0, :N_EXPERTS].astype(jnp.int32)
    padded = ((counts + t - 1) // t) * t
    ends = jnp.cumsum(padded)
    pos = (ends - padded)[e12] + rank
    n_active = ends[-1] // t
    tile_start = jnp.arange(MOE_ROWS // t, dtype=jnp.int32) * t
    tile_expert = jnp.sum(tile_start[:, None] >= ends[None, :], axis=1).astype(jnp.int32)
    last_expert = jnp.sum((n_active - 1) * t >= ends).astype(jnp.int32)
    tile_expert = jnp.minimum(tile_expert, last_expert)
    pos_blk = pos.reshape(N_TOK // MOE_TM, MOE_TM, 2).transpose(0, 2, 1)
    return pos_blk, tile_expert, n_active.reshape(1).astype(jnp.int32)


def _row(ref, r):
    return ref.at[pl.ds(r, 1), :]


def _dispatch_kernel(pos_ref, x_ref, xs_in_hbm, xs_hbm, sem):
    del xs_in_hbm

    def issue(r, c):
        pltpu.make_async_copy(_row(x_ref, r), _row(xs_hbm, pos_ref[0, r]), sem).start()
        pltpu.make_async_copy(_row(x_ref, r), _row(xs_hbm, pos_ref[1, r]), sem).start()
        return c

    def drain(r, c):
        pltpu.make_async_copy(_row(x_ref, 0), _row(xs_hbm, 0), sem).wait()
        pltpu.make_async_copy(_row(x_ref, 0), _row(xs_hbm, 0), sem).wait()
        return c

    lax.fori_loop(0, MOE_TM, issue, 0, unroll=8)
    lax.fori_loop(0, MOE_TM, drain, 0, unroll=8)


def _dispatch(pos_blk, x):
    zeros = jnp.zeros((MOE_ROWS, D_MODEL), F32)
    return pl.pallas_call(
        _dispatch_kernel,
        grid=(N_TOK // MOE_TM,),
        in_specs=[
            pl.BlockSpec((None, 2, MOE_TM), lambda i: (i, 0, 0), memory_space=pltpu.SMEM),
            pl.BlockSpec((MOE_TM, D_MODEL), lambda i: (i, 0)),
            pl.BlockSpec(memory_space=pl.ANY),
        ],
        out_specs=pl.BlockSpec(memory_space=pl.ANY),
        out_shape=jax.ShapeDtypeStruct((MOE_ROWS, D_MODEL), F32),
        scratch_shapes=[pltpu.SemaphoreType.DMA(())],
        input_output_aliases={2: 0},
        compiler_params=_cparams(("arbitrary",)),
        name="moe_dispatch",
    )(pos_blk, x, zeros)


def _moe_ffn_kernel(te_ref, na_ref, xs_ref, gn_ref, w1_ref, w3_ref, w2_ref, ys_ref, hb_scr, acc_scr):
    del te_ref
    f = pl.program_id(1)
    active = pl.program_id(0) < na_ref[0]

    @pl.when(active & (f == 0))
    def _():
        hb_scr[...] = _rmsnorm(xs_ref[...], gn_ref[...]).astype(BF16)
        acc_scr[...] = jnp.zeros_like(acc_scr)

    @pl.when(active)
    def _():
        h = hb_scr[...]
        a = jnp.dot(h, w1_ref[...], preferred_element_type=F32)
        b = jnp.dot(h, w3_ref[...], preferred_element_type=F32)
        t = (a * jax.nn.sigmoid(a)) * b
        acc_scr[...] += jnp.dot(t.astype(BF16), w2_ref[...], preferred_element_type=F32)

    @pl.when(f == pl.num_programs(1) - 1)
    def _():
        ys_ref[...] = jnp.where(active, acc_scr[...], 0.0)


def _moe_ffn(tile_expert, n_active, xs, g, w1, w3, w2):
    t, tf = MOE_T, MOE_TF
    nf = w1.shape[2] // tf

    def wspec(shape, order):
        def index(r, f, te, na):
            ff = jnp.where(r < na[0], f, nf - 1)
            return (te[r], ff, 0) if order else (te[r], 0, ff)
        return pl.BlockSpec(shape, index)

    grid_spec = pltpu.PrefetchScalarGridSpec(
        num_scalar_prefetch=2,
        grid=(MOE_ROWS // t, nf),
        in_specs=[
            pl.BlockSpec((t, D_MODEL), lambda r, f, te, na: (r, 0)),
            pl.BlockSpec((1, D_MODEL), lambda r, f, te, na: (0, 0)),
            wspec((None, D_MODEL, tf), False),
            wspec((None, D_MODEL, tf), False),
            wspec((None, tf, D_MODEL), True),
        ],
        out_specs=pl.BlockSpec((t, D_MODEL), lambda r, f, te, na: (r, 0)),
        scratch_shapes=[pltpu.VMEM((t, D_MODEL), BF16), pltpu.VMEM((t, D_MODEL), F32)],
    )
    return pl.pallas_call(
        _moe_ffn_kernel,
        grid_spec=grid_spec,
        out_shape=jax.ShapeDtypeStruct((MOE_ROWS, D_MODEL), F32),
        compiler_params=_cparams(("parallel", "arbitrary")),
        name="moe_ffn",
    )(tile_expert, n_active, xs, g.reshape(1, D_MODEL), w1, w3, w2)


def _combine_kernel(pos_ref, route_ref, x_ref, gf_ref, ys_hbm, *rest, final_norm):
    out_refs, (y1_buf, y2_buf, sem) = rest[:-3], rest[-3:]
    def issue(r, c):
        pltpu.make_async_copy(_row(ys_hbm, pos_ref[0, r]), _row(y1_buf, r), sem).start()
        pltpu.make_async_copy(_row(ys_hbm, pos_ref[1, r]), _row(y2_buf, r), sem).start()
        return c

    def drain(r, c):
        pltpu.make_async_copy(_row(ys_hbm, 0), _row(y1_buf, 0), sem).wait()
        pltpu.make_async_copy(_row(ys_hbm, 0), _row(y2_buf, 0), sem).wait()
        return c

    lax.fori_loop(0, MOE_TM, issue, 0, unroll=8)
    lax.fori_loop(0, MOE_TM, drain, 0, unroll=8)
    y = x_ref[...] + (route_ref[:, 0:1] * y1_buf[...] + route_ref[:, 1:2] * y2_buf[...])
    if final_norm:
        y = _rmsnorm(y, gf_ref[...])
        is_prompt = pl.program_id(0) < N_PROMPT // MOE_TM

        @pl.when(is_prompt)
        def _():
            out_refs[0][...] = y

        @pl.when(jnp.logical_not(is_prompt))
        def _():
            out_refs[1][...] = y
    else:
        out_refs[0][...] = y


def _combine(pos_blk, route, x, g_final, ys, *, final_norm):
    tm = MOE_TM
    n_p = N_PROMPT // tm
    if final_norm:
        out_specs = [pl.BlockSpec((tm, D_MODEL), lambda i: (jnp.minimum(i, n_p - 1), 0)),
                     pl.BlockSpec((tm, D_MODEL), lambda i: (jnp.maximum(i - n_p, 0), 0))]
        out_shape = [jax.ShapeDtypeStruct((N_PROMPT, D_MODEL), F32),
                     jax.ShapeDtypeStruct((N_SAMPLE, D_MODEL), F32)]
    else:
        out_specs = pl.BlockSpec((tm, D_MODEL), lambda i: (i, 0))
        out_shape = jax.ShapeDtypeStruct((N_TOK, D_MODEL), F32)
    return pl.pallas_call(
        functools.partial(_combine_kernel, final_norm=final_norm),
        grid=(N_TOK // tm,),
        in_specs=[
            pl.BlockSpec((None, 2, tm), lambda i: (i, 0, 0), memory_space=pltpu.SMEM),
            pl.BlockSpec((tm, LANES), lambda i: (i, 0)),
            pl.BlockSpec((tm, D_MODEL), lambda i: (i, 0)),
            pl.BlockSpec((1, D_MODEL), lambda i: (0, 0)),
            pl.BlockSpec(memory_space=pl.ANY),
        ],
        out_specs=out_specs,
        out_shape=out_shape,
        scratch_shapes=[pltpu.VMEM((tm, D_MODEL), F32), pltpu.VMEM((tm, D_MODEL), F32),
                        pltpu.SemaphoreType.DMA(())],
        compiler_params=_cparams(("arbitrary",)),
        name="moe_combine",
    )(pos_blk, route, x, g_final.reshape(1, D_MODEL), ys)


def _ffn_kernel(h_ref, x_ref, w1_ref, w3_ref, w2_ref, gf_ref, o_ref, acc_scr, *, final_norm):
    f = pl.program_id(1)

    @pl.when(f == 0)
    def _():
        acc_scr[...] = jnp.zeros_like(acc_scr)

    h = h_ref[...]
    a = jnp.dot(h, w1_ref[...], preferred_element_type=F32)
    b = jnp.dot(h, w3_ref[...], preferred_element_type=F32)
    t = (a * jax.nn.sigmoid(a)) * b
    acc_scr[...] += jnp.dot(t.astype(BF16), w2_ref[...], preferred_element_type=F32)

    @pl.when(f == pl.num_programs(1) - 1)
    def _():
        y = x_ref[...] + acc_scr[...]
        if final_norm:
            y = _rmsnorm(y, gf_ref[...])
        o_ref[...] = y


def _ffn(h2, x, w1, w3, w2, g_final, *, final_norm):
    tm, tf = 1280, 256
    d_ff = w1.shape[1]
    return pl.pallas_call(
        functools.partial(_ffn_kernel, final_norm=final_norm),
        grid=(N_TOK // tm, d_ff // tf),
        in_specs=[
            pl.BlockSpec((tm, D_MODEL), lambda i, f: (i, 0)),
            pl.BlockSpec((tm, D_MODEL), lambda i, f: (i, 0)),
            pl.BlockSpec((D_MODEL, tf), lambda i, f: (0, f)),
            pl.BlockSpec((D_MODEL, tf), lambda i, f: (0, f)),
            pl.BlockSpec((tf, D_MODEL), lambda i, f: (f, 0)),
            pl.BlockSpec((1, D_MODEL), lambda i, f: (0, 0)),
        ],
        out_specs=pl.BlockSpec((tm, D_MODEL), lambda i, f: (i, 0)),
        out_shape=jax.ShapeDtypeStruct((N_TOK, D_MODEL), F32),
        scratch_shapes=[pltpu.VMEM((tm, D_MODEL), F32)],
        compiler_params=_cparams(("parallel", "arbitrary")),
        name="ffn_dense",
    )(h2, x, w1, w3, w2, g_final.reshape(1, D_MODEL))


def _pad_hist(hist, width):
    nseq, _, c = hist.shape
    return jnp.concatenate([jnp.zeros((nseq, SUBLANES - (width - 1), c), hist.dtype), hist], axis=1)


def _block_diag(w):
    h, d, _ = w.shape
    eye = jnp.eye(h, dtype=w.dtype)
    return (eye[:, None, :, None] * w[:, :, None, :]).reshape(h * d, h * d)


def kernel(x_prompt, x_sample, cache_k, cache_v, state_conv_a, state_conv_b, state_h, page_table,
           norm_mix, w_in, conv_a_w, conv_b_w, conv_b_bias, rg_w_r, rg_b_r, rg_w_i, rg_b_i,
           rg_lambda, sb_bias, w_br_a, w_br_b, w_br_c, w_out, norm_ffn, ffn_w1, ffn_w3, ffn_w2,
           moe_router, moe_w1, moe_w3, moe_w2, norm_final):
    depth = w_in.shape[0]
    n_phys = cache_k.shape[1]
    cache_k4 = cache_k.transpose(0, 1, 3, 4, 2).reshape(depth, n_phys, W_C, PAGE_SIZE)
    cache_v4 = cache_v.transpose(0, 1, 3, 4, 2).reshape(depth, n_phys, W_C, PAGE_SIZE)
    x = jnp.concatenate([x_prompt.reshape(N_PROMPT, D_MODEL), x_sample.reshape(N_SAMPLE, D_MODEL)])
    ri = lax.broadcasted_iota(jnp.int32, (ATT_TK, ATT_TK), 0)
    ci = lax.broadcasted_iota(jnp.int32, (ATT_TK, ATT_TK), 1)
    u = (ri > ci).astype(BF16)
    zeros8 = jnp.zeros((BATCH, SUBLANES, W_A), F32)

    st_p, st_s = [], []
    for l in range(depth):
        lw = {
            "conv_a_w": conv_a_w[l], "conv_b_w": conv_b_w[l], "conv_b_bias": conv_b_bias[l][None],
            "rg_w_r": _block_diag(rg_w_r[l]).astype(BF16), "rg_b_r": rg_b_r[l].reshape(1, W_B),
            "rg_w_i": _block_diag(rg_w_i[l]).astype(BF16), "rg_b_i": rg_b_i[l].reshape(1, W_B),
            "rg_lambda": rg_lambda[l][None],
            "w_br_a": w_br_a[l].astype(BF16), "w_br_b": w_br_b[l].astype(BF16),
            "w_br_c": w_br_c[l].astype(BF16), "w_out": w_out[l].astype(BF16),
            "norm_ffn": norm_ffn[l][None],
        }
        proj = _in_proj(x, norm_mix[l], w_in[l].astype(BF16))

        y_ab, sa_p, sb_p, hl_p = _mixers(proj, zeros8, zeros8, zeros8, lw, jnp.zeros((N_TOK, W_C), F32),
                                         nseq=BATCH, t=SEQ, row0=0, first_pos_zero=True)
        h0_8 = jnp.broadcast_to(state_h[l][:, None, :], (DEC_BATCH, SUBLANES, W_B))
        y_ab, sa_s, sb_s, hl_s = _mixers(proj, _pad_hist(state_conv_a[l], CONV_A),
                                         _pad_hist(state_conv_b[l], CONV_B), h0_8, lw, y_ab,
                                         nseq=DEC_BATCH, t=DEC_SEQ, row0=N_PROMPT,
                                         first_pos_zero=False)

        y_c, kt, vt = _attn_prompt(proj, sb_bias[l], u, jnp.zeros((N_TOK, W_C), F32))
        y_c = _attn_sample(proj, y_c, cache_k4, cache_v4, page_table, l, sb_bias[l], u)

        x, h2 = _merge(y_ab, y_c, proj, x, lw)

        last = l == depth - 1
        if l % 2 == 0:
            i = l // 2
            x = _ffn(h2, x, ffn_w1[i].astype(BF16), ffn_w3[i].astype(BF16), ffn_w2[i].astype(BF16),
                     norm_final, final_norm=last)
        else:
            i = l // 2
            router_pad = jnp.pad(moe_router[i], ((0, 0), (0, LANES - N_EXPERTS)))
            route, cnt = _router(x, norm_ffn[l], router_pad, u)
            pos_blk, tile_expert, n_active = _route_plan(route, cnt)
            xs = _dispatch(pos_blk, x)
            ys = _moe_ffn(tile_expert, n_active, xs, norm_ffn[l], moe_w1[i].astype(BF16),
                          moe_w3[i].astype(BF16), moe_w2[i].astype(BF16))
            x = _combine(pos_blk, route, x, norm_final, ys, final_norm=last)
            if last:
                y_prompt, y_sample = x

        kp = kt.reshape(BATCH, H_C, DH_C, SEQ).transpose(0, 3, 1, 2)
        vp = vt.reshape(BATCH, H_C, DH_C, SEQ).transpose(0, 3, 1, 2)
        ks = proj[N_PROMPT:, COL_K:COL_K + W_C].reshape(DEC_BATCH, DEC_SEQ, H_C, DH_C)
        vs = proj[N_PROMPT:, COL_V:COL_V + W_C].reshape(DEC_BATCH, DEC_SEQ, H_C, DH_C)
        st_p.append((kp, vp, sa_p[:, SUBLANES - (CONV_A - 1):], sb_p[:, SUBLANES - (CONV_B - 1):],
                     hl_p[:, 0]))
        st_s.append((ks, vs, sa_s[:, SUBLANES - (CONV_A - 1):], sb_s[:, SUBLANES - (CONV_B - 1):],
                     hl_s[:, 0]))

    if depth % 2 == 1:
        y_prompt, y_sample = x[:N_PROMPT], x[N_PROMPT:]
    y_prompt = y_prompt.reshape(BATCH, SEQ, D_MODEL)
    y_sample = y_sample.reshape(DEC_BATCH, DEC_SEQ, D_MODEL)
    stack = lambda sts, k: jnp.stack([s[k] for s in sts])
    return (y_prompt, y_sample,
            stack(st_p, 0), stack(st_p, 1), stack(st_p, 2), stack(st_p, 3), stack(st_p, 4),
            stack(st_s, 0), stack(st_s, 1), stack(st_s, 2), stack(st_s, 3), stack(st_s, 4))
```

```python
import functools

import jax
import jax.numpy as jnp
from jax import lax
from jax.experimental import pallas as pl
from jax.experimental.pallas import tpu as pltpu

F32 = jnp.float32
BF16 = jnp.bfloat16

D_MODEL = 1024
BATCH = 4
SEQ = 4096
DEC_BATCH = 32
DEC_SEQ = 8
PAGE_SIZE = 128
W_A = 256
W_B = 256
H_B = 4
DH_B = 64
CONV_A = 3
CONV_B = 4
RG_C = 8.0
H_C = 8
DH_C = 64
W_C = 512
IN_WIDTH = 5632
N_EXPERTS = 8
EPS = 1e-6
LOG2E = 1.4426950408889634

N_PROMPT = BATCH * SEQ
N_SAMPLE = DEC_BATCH * DEC_SEQ
N_TOK = N_PROMPT + N_SAMPLE

COL_Q = 1024
COL_K = 1536
COL_V = 2048
COL_G = 2560

SUBLANES = 8
LANES = 128
VMEM_LIMIT = 56 * 1024 * 1024

ATT_TQ = 512
ATT_TK = 256
PAGES_PER_STEP = 16

MOE_TM = 256
ROUTER_TM = 1280
MOE_T = 512
MOE_TF = 1792
MOE_ROWS = -(-(2 * N_TOK + N_EXPERTS * (MOE_T - 1)) // MOE_T) * MOE_T


def _cparams(sem):
    return pltpu.CompilerParams(dimension_semantics=sem, vmem_limit_bytes=VMEM_LIMIT)


def _rmsnorm(x, g):
    y = x * lax.rsqrt(jnp.mean(x * x, axis=-1, keepdims=True) + EPS)
    return y * g


def _inproj_kernel(x_ref, g_ref, w_ref, o_ref, h_scr):
    @pl.when(pl.program_id(1) == 0)
    def _():
        h_scr[...] = _rmsnorm(x_ref[...], g_ref[...]).astype(BF16)

    o_ref[...] = jnp.dot(h_scr[...], w_ref[...], preferred_element_type=F32).astype(o_ref.dtype)


def _in_proj(x, g, w_bf16, tn, out_dtype):
    tm = 1280
    width = w_bf16.shape[1]
    return pl.pallas_call(
        _inproj_kernel,
        grid=(N_TOK // tm, width // tn),
        in_specs=[
            pl.BlockSpec((tm, D_MODEL), lambda i, j: (i, 0)),
            pl.BlockSpec((1, D_MODEL), lambda i, j: (0, 0)),
            pl.BlockSpec((D_MODEL, tn), lambda i, j: (0, j)),
        ],
        out_specs=pl.BlockSpec((tm, tn), lambda i, j: (i, j)),
        out_shape=jax.ShapeDtypeStruct((N_TOK, width), out_dtype),
        scratch_shapes=[pltpu.VMEM((tm, D_MODEL), BF16)],
        compiler_params=_cparams(("parallel", "arbitrary")),
        name="in_proj",
    )(x, g.reshape(1, D_MODEL), w_bf16)


def _softplus(x):
    return jnp.maximum(x, 0.0) + jnp.log1p(jnp.exp(-jnp.abs(x)))


def _mixers_kernel(p_ref, ha_ref, hb_ref, h0_ref, wa_ref, wb_ref, bb_ref, wr_ref, br_ref,
                   wi_ref, bi_ref, lam_ref, y_prev_hbm, y_ref, sa_ref, sb_ref, hl_ref, ua_buf, xb_buf,
                   hc_scr, *, tt, first_pos_zero):
    del y_prev_hbm
    ti = pl.program_id(1)

    @pl.when(ti == 0)
    def _():
        ua_buf[0:SUBLANES, :] = ha_ref[...]
        xb_buf[0:SUBLANES, :] = hb_ref[...]
        hc_scr[...] = h0_ref[...]

    bg = p_ref[:, 0:W_A]
    cg = p_ref[:, W_A:2 * W_A]
    xa = p_ref[:, 2 * W_A:3 * W_A]
    xb = p_ref[:, 3 * W_A:3 * W_A + W_B]

    ua = cg * xa
    ua_buf[SUBLANES:SUBLANES + tt, :] = ua
    ya = ua_buf[pl.ds(SUBLANES - 2, tt), :] * wa_ref[0:1, :]
    ya = ya + ua_buf[pl.ds(SUBLANES - 1, tt), :] * wa_ref[1:2, :]
    ya = ya + ua * wa_ref[2:3, :]
    y_ref[:, 0:W_A] = bg * ya

    xb_buf[SUBLANES:SUBLANES + tt, :] = xb
    xc = xb_buf[pl.ds(SUBLANES - 3, tt), :] * wb_ref[0:1, :]
    xc = xc + xb_buf[pl.ds(SUBLANES - 2, tt), :] * wb_ref[1:2, :]
    xc = xc + xb_buf[pl.ds(SUBLANES - 1, tt), :] * wb_ref[2:3, :]
    xc = xc + xb * wb_ref[3:4, :]
    xc = xc + bb_ref[...]
    xcb = xc.astype(BF16)
    r = jax.nn.sigmoid(jnp.dot(xcb, wr_ref[...], preferred_element_type=F32) + br_ref[...])
    ig = jax.nn.sigmoid(jnp.dot(xcb, wi_ref[...], preferred_element_type=F32) + bi_ref[...])
    log_a = (-RG_C * r) * _softplus(-lam_ref[...])
    a = jnp.exp(log_a)
    mult = jnp.sqrt(1.0 - jnp.exp(2.0 * log_a))
    row = lax.broadcasted_iota(jnp.int32, (tt, W_B), 0)
    if first_pos_zero:
        mult = jnp.where((row == 0) & (ti == 0), 1.0, mult)
    b = mult * (ig * xc)

    s = 1
    while s < tt:
        keep = row >= s
        a_sh = pltpu.roll(a, s, 0)
        b_sh = pltpu.roll(b, s, 0)
        b = jnp.where(keep, a * b_sh + b, b)
        a = jnp.where(keep, a * a_sh, a)
        s *= 2
    h = a * hc_scr[0:1, :] + b
    y_ref[:, W_A:W_A + W_B] = h

    last8_a = ua_buf[tt:tt + SUBLANES, :]
    last8_b = xb_buf[tt:tt + SUBLANES, :]
    hlast = jnp.broadcast_to(h[tt - 1:tt, :], (SUBLANES, W_B))
    ua_buf[0:SUBLANES, :] = last8_a
    xb_buf[0:SUBLANES, :] = last8_b
    hc_scr[...] = hlast
    sa_ref[...] = last8_a
    sb_ref[...] = last8_b
    hl_ref[...] = hlast


def _mixers(proj, hist_a8, hist_b8, h0_8, lw, y_prev, *, nseq, t, row0, first_pos_zero):
    tt = min(t, 512)
    nt = t // tt
    rb0 = row0 // tt
    kern = functools.partial(_mixers_kernel, tt=tt, first_pos_zero=first_pos_zero)
    seq3 = pl.BlockSpec((None, SUBLANES, W_A), lambda s, i: (s, 0, 0))
    full = lambda shape: pl.BlockSpec(shape, lambda s, i: (0,) * len(shape))
    in_specs = [
        pl.BlockSpec((tt, 4 * W_A), lambda s, i: (rb0 + s * nt + i, 0)),
        seq3, seq3, seq3,
        full((CONV_A, W_A)), full((CONV_B, W_B)), full((1, W_B)),
        full((W_B, W_B)), full((1, W_B)), full((W_B, W_B)), full((1, W_B)), full((1, W_B)),
        pl.BlockSpec(memory_space=pl.ANY),
    ]
    args = [proj, hist_a8, hist_b8, h0_8, lw["conv_a_w"], lw["conv_b_w"], lw["conv_b_bias"],
            lw["rg_w_r"], lw["rg_b_r"], lw["rg_w_i"], lw["rg_b_i"], lw["rg_lambda"], y_prev]
    st = jax.ShapeDtypeStruct((nseq, SUBLANES, W_A), F32)
    return pl.pallas_call(
        kern,
        grid=(nseq, nt),
        in_specs=in_specs,
        out_specs=[
            pl.BlockSpec((tt, W_A + W_B), lambda s, i: (rb0 + s * nt + i, 0)),
            seq3, seq3, seq3,
        ],
        out_shape=[jax.ShapeDtypeStruct((N_TOK, W_A + W_B), F32), st, st, st],
        scratch_shapes=[
            pltpu.VMEM((SUBLANES + tt, W_A), F32),
            pltpu.VMEM((SUBLANES + tt, W_B), F32),
            pltpu.VMEM((SUBLANES, W_B), F32),
        ],
        input_output_aliases={len(args) - 1: 0},
        compiler_params=_cparams(("parallel", "arbitrary")),
        name="mixers_t%d" % t,
    )(*args)


def _log2_sigmoid_pair(z2):
    m = jnp.minimum(z2, 0.0)
    n = m - z2
    l2 = jnp.log2(1.0 + jnp.exp2(m + n))
    return m - l2, n - l2


def _sb_chunk(qm, bias, kc, vc, u, carry, mask=None):
    z2 = lax.dot_general(qm, kc, (((1,), (1,)), ((), ())), preferred_element_type=F32) + bias
    ls, lk = _log2_sigmoid_pair(z2)
    if mask is not None:
        lk = jnp.where(mask, lk, 0.0)
    after = jnp.dot(lk.astype(BF16), u, preferred_element_type=F32)
    total = after[:, 0:1] + lk[:, 0:1]
    w = jnp.exp2(ls + after + carry)
    if mask is not None:
        w = jnp.where(mask, w, 0.0)
    return jnp.dot(w.astype(BF16), vc, preferred_element_type=F32), total


def _sb_diagonal(qm, bias, k1, v1, k0, v0, u):
    c = u.shape[0]
    row = lax.broadcasted_iota(jnp.int32, (c, c), 0)
    col = lax.broadcasted_iota(jnp.int32, (c, c), 1)
    causal = col < row
    q_up, q_lo = qm[:c], qm[c:]
    pv_l1, t_l1 = _sb_chunk(q_lo, bias, k1, v1, u, 0.0, causal)
    pv_l0, t_l0 = _sb_chunk(q_lo, bias, k0, v0, u, t_l1)
    pv_u, t_u = _sb_chunk(q_up, bias, k0, v0, u, 0.0, causal)
    return (jnp.concatenate([pv_u, pv_l1 + pv_l0], axis=0),
            jnp.concatenate([t_u, t_l1 + t_l0], axis=0))


def _attn_prompt_kernel(bias_ref, q_ref, k_ref, v_ref, u_ref, y_prev_hbm, o_ref, kt_ref, vt_ref,
                        kb_scr, vb_scr, acc_scr, cr_scr):
    del y_prev_hbm
    hp = pl.program_id(1)
    qi = pl.program_id(2)
    tq, tk = ATT_TQ, ATT_TK
    nblk = tq // tk

    @pl.when(qi == 0)
    def _():
        kb_scr[...] = k_ref[...].astype(BF16)
        vb_scr[...] = v_ref[...].astype(BF16)
        for c in range(SEQ // tq):
            rows = slice(c * tq, (c + 1) * tq)
            kt_ref[:, rows] = k_ref[rows, :].T
            vt_ref[:, rows] = v_ref[rows, :].T

    q = q_ref[...] * (LOG2E * DH_C ** -0.5)
    lane = lax.broadcasted_iota(jnp.int32, (tq, LANES), 1)
    qm = [jnp.where(lane < DH_C, q, 0.0).astype(BF16), jnp.where(lane >= DH_C, q, 0.0).astype(BF16)]
    bias = [bias_ref[2 * hp] * LOG2E, bias_ref[2 * hp + 1] * LOG2E]
    u = u_ref[...]

    st1 = pl.multiple_of((qi * nblk + 1) * tk, tk)
    st0 = pl.multiple_of(qi * nblk * tk, tk)
    for hh in range(2):
        acc, carry = _sb_diagonal(qm[hh], bias[hh], kb_scr[pl.ds(st1, tk), :], vb_scr[pl.ds(st1, tk), :],
                                  kb_scr[pl.ds(st0, tk), :], vb_scr[pl.ds(st0, tk), :], u)
        acc_scr[hh] = acc
        cr_scr[hh] = jnp.broadcast_to(carry, (tq, LANES))

    def pair(blk, hh):
        carry = cr_scr[hh, :, 0:1]
        st1 = pl.multiple_of(blk * tk, tk)
        st0 = pl.multiple_of((blk - 1) * tk, tk)
        pv1, t1 = _sb_chunk(qm[hh], bias[hh], kb_scr[pl.ds(st1, tk), :], vb_scr[pl.ds(st1, tk), :],
                            u, carry)
        pv0, t0 = _sb_chunk(qm[hh], bias[hh], kb_scr[pl.ds(st0, tk), :], vb_scr[pl.ds(st0, tk), :],
                            u, carry + t1)
        acc_scr[hh] += pv1 + pv0
        cr_scr[hh] += jnp.broadcast_to(t1 + t0, (tq, LANES))

    def body(it, _):
        for hh in range(2):
            pair((qi - it) * nblk - 1, hh)
        return 0

    assert nblk == 2
    lax.fori_loop(0, qi, body, 0)
    o_ref[...] = jnp.where(lane < DH_C, acc_scr[0], acc_scr[1])


def _attn_prompt(proj, sb_bias, u, y_prev):
    tq = ATT_TQ
    nq = SEQ // tq
    grid_spec = pltpu.PrefetchScalarGridSpec(
        num_scalar_prefetch=1,
        grid=(BATCH, H_C // 2, nq),
        in_specs=[
            pl.BlockSpec((tq, LANES), lambda b, hp, qi, s: (b * nq + qi, COL_Q // LANES + hp)),
            pl.BlockSpec((SEQ, LANES), lambda b, hp, qi, s: (b, COL_K // LANES + hp)),
            pl.BlockSpec((SEQ, LANES), lambda b, hp, qi, s: (b, COL_V // LANES + hp)),
            pl.BlockSpec((ATT_TK, ATT_TK), lambda b, hp, qi, s: (0, 0)),
            pl.BlockSpec(memory_space=pl.ANY),
        ],
        out_specs=[
            pl.BlockSpec((tq, LANES), lambda b, hp, qi, s: (b * nq + qi, hp)),
            pl.BlockSpec((None, LANES, SEQ), lambda b, hp, qi, s: (b, hp, 0)),
            pl.BlockSpec((None, LANES, SEQ), lambda b, hp, qi, s: (b, hp, 0)),
        ],
        scratch_shapes=[
            pltpu.VMEM((SEQ, LANES), BF16),
            pltpu.VMEM((SEQ, LANES), BF16),
            pltpu.VMEM((2, tq, LANES), F32),
            pltpu.VMEM((2, tq, LANES), F32),
        ],
    )
    kv_t = jax.ShapeDtypeStruct((BATCH, W_C, SEQ), F32)
    return pl.pallas_call(
        _attn_prompt_kernel,
        grid_spec=grid_spec,
        out_shape=[jax.ShapeDtypeStruct((N_TOK, W_C), F32), kv_t, kv_t],
        input_output_aliases={5: 0},
        compiler_params=_cparams(("parallel", "parallel", "arbitrary")),
        name="attn_prompt",
    )(sb_bias, proj, proj, proj, u, y_prev)


def _attn_sample_kernel(pt_ref, bias_ref, q_ref, kn_ref, vn_ref, u_ref, *rest):
    npg = PAGES_PER_STEP
    k_refs = rest[:npg]
    v_refs = rest[npg:2 * npg]
    (o_ref, qbd_scr, bias_scr, kn_scr, vn_scr, kcat_scr, vcat_scr, acc_scr,
     cr_scr) = rest[2 * npg + 1:]
    j = pl.program_id(1)
    nrow = H_C * DEC_SEQ
    u = u_ref[0:PAGE_SIZE, 0:PAGE_SIZE]

    @pl.when(j == 0)
    def _():
        row = lax.broadcasted_iota(jnp.int32, (nrow, W_C), 0)
        lane = lax.broadcasted_iota(jnp.int32, (nrow, W_C), 1)
        q8 = q_ref[...] * (LOG2E * DH_C ** -0.5)
        qt = jnp.concatenate([q8] * H_C, axis=0)
        qbd_scr[...] = jnp.where(row // DEC_SEQ == lane // DH_C, qt, 0.0).astype(BF16)
        rowb = lax.broadcasted_iota(jnp.int32, (nrow, LANES), 0)
        bias = jnp.zeros((nrow, LANES), F32)
        for h in range(H_C):
            bias = jnp.where(rowb // DEC_SEQ == h, bias_ref[h] * LOG2E, bias)
        bias_scr[...] = bias
        kn_scr[...] = jnp.zeros((PAGE_SIZE, W_C), BF16)
        vn_scr[...] = jnp.zeros((PAGE_SIZE, W_C), BF16)
        kn_scr[0:DEC_SEQ, :] = kn_ref[...].astype(BF16)
        vn_scr[0:DEC_SEQ, :] = vn_ref[...].astype(BF16)
        rowc = lax.broadcasted_iota(jnp.int32, (nrow, PAGE_SIZE), 0)
        colc = lax.broadcasted_iota(jnp.int32, (nrow, PAGE_SIZE), 1)
        pv, total = _sb_chunk(qbd_scr[...], bias_scr[:, 0:1], kn_scr[...], vn_scr[...], u, 0.0,
                              mask=colc < rowc % DEC_SEQ)
        acc_scr[...] = pv
        cr_scr[...] = jnp.broadcast_to(total, (nrow, LANES))

    for p in range(npg):
        kcat_scr[:, p * PAGE_SIZE:(p + 1) * PAGE_SIZE] = k_refs[p][...].astype(BF16)
        vcat_scr[:, p * PAGE_SIZE:(p + 1) * PAGE_SIZE] = v_refs[p][...].astype(BF16)
    z2 = jnp.dot(qbd_scr[...], kcat_scr[...], preferred_element_type=F32)
    ls, lk = _log2_sigmoid_pair(z2 + bias_scr[:, 0:1])
    lkb = lk.astype(BF16)
    carry = cr_scr[:, 0:1]
    afters = []
    for p in range(npg):
        sl = slice(p * PAGE_SIZE, (p + 1) * PAGE_SIZE)
        a = jnp.dot(lkb[:, sl], u, preferred_element_type=F32)
        afters.append(a + carry)
        carry = carry + (a[:, 0:1] + lk[:, p * PAGE_SIZE:p * PAGE_SIZE + 1])
    w = jnp.exp2(ls + jnp.concatenate(afters, axis=1)).astype(BF16)
    acc_scr[...] += lax.dot_general(w, vcat_scr[...], (((1,), (1,)), ((), ())),
                                    preferred_element_type=F32)
    cr_scr[...] = jnp.broadcast_to(carry, (nrow, LANES))

    @pl.when(j == pl.num_programs(1) - 1)
    def _():
        lane = lax.broadcasted_iota(jnp.int32, (DEC_SEQ, W_C), 1)
        out = jnp.zeros((DEC_SEQ, W_C), F32)
        for h in range(H_C):
            out = out + jnp.where(lane // DH_C == h, acc_scr[h * DEC_SEQ:(h + 1) * DEC_SEQ, :], 0.0)
        o_ref[...] = out


def _attn_sample(proj, y_c, cache_k4, cache_v4, page_table, layer, sb_bias, u):
    npg = PAGES_PER_STEP
    n_pages = page_table.shape[1]
    nsteps = n_pages // npg
    rb0 = N_PROMPT // DEC_SEQ
    nrow = H_C * DEC_SEQ

    def row_spec(colblk):
        return pl.BlockSpec((DEC_SEQ, W_C), lambda b, j, pt, s: (rb0 + b, colblk))

    def page_spec(p):
        return pl.BlockSpec((None, None, W_C, PAGE_SIZE),
                            lambda b, j, pt, s: (layer, pt[b, n_pages - 1 - (j * npg + p)], 0, 0))

    grid_spec = pltpu.PrefetchScalarGridSpec(
        num_scalar_prefetch=2,
        grid=(DEC_BATCH, nsteps),
        in_specs=([row_spec(COL_Q // W_C), row_spec(COL_K // W_C), row_spec(COL_V // W_C),
                   pl.BlockSpec((ATT_TK, ATT_TK), lambda b, j, pt, s: (0, 0))]
                  + [page_spec(p) for p in range(npg)]
                  + [page_spec(p) for p in range(npg)]
                  + [pl.BlockSpec(memory_space=pl.ANY)]),
        out_specs=pl.BlockSpec((DEC_SEQ, W_C), lambda b, j, pt, s: (rb0 + b, 0)),
        scratch_shapes=[
            pltpu.VMEM((nrow, W_C), BF16),
            pltpu.VMEM((nrow, LANES), F32),
            pltpu.VMEM((PAGE_SIZE, W_C), BF16),
            pltpu.VMEM((PAGE_SIZE, W_C), BF16),
            pltpu.VMEM((W_C, npg * PAGE_SIZE), BF16),
            pltpu.VMEM((W_C, npg * PAGE_SIZE), BF16),
            pltpu.VMEM((nrow, W_C), F32),
            pltpu.VMEM((nrow, LANES), F32),
        ],
    )
    n_in = 2 + 4 + 2 * npg
    return pl.pallas_call(
        _attn_sample_kernel,
        grid_spec=grid_spec,
        out_shape=jax.ShapeDtypeStruct((N_TOK, W_C), F32),
        input_output_aliases={n_in: 0},
        compiler_params=_cparams(("parallel", "arbitrary")),
        name="attn_sample",
    )(page_table, sb_bias, proj, proj, proj, u, *([cache_k4] * npg), *([cache_v4] * npg), y_c)


def _merge_kernel(yab_ref, yc_ref, ga_ref, gb_ref, gc_ref, x_ref, wa_ref, wb_ref, wc_ref, wo_ref,
                  gn_ref, xo_ref, h2_ref):
    ya = yab_ref[:, 0:W_A].astype(BF16)
    yb = yab_ref[:, W_A:W_A + W_B].astype(BF16)
    yc = yc_ref[...].astype(BF16)
    gate = lambda ref: jax.nn.sigmoid(ref[...].astype(F32))
    m = gate(ga_ref) * jnp.dot(ya, wa_ref[...], preferred_element_type=F32)
    m = m + gate(gb_ref) * jnp.dot(yb, wb_ref[...], preferred_element_type=F32)
    m = m + gate(gc_ref) * jnp.dot(yc, wc_ref[...], preferred_element_type=F32)
    xn = x_ref[...] + jnp.dot(m.astype(BF16), wo_ref[...], preferred_element_type=F32)
    xo_ref[...] = xn
    h2_ref[...] = _rmsnorm(xn, gn_ref[...]).astype(BF16)


def _merge(y_ab, y_c, gates, x, lw):
    tm = 640
    rows = lambda w, c: pl.BlockSpec((tm, w), lambda i: (i, c))
    full = lambda shape: pl.BlockSpec(shape, lambda i: (0, 0))
    return pl.pallas_call(
        _merge_kernel,
        grid=(N_TOK // tm,),
        in_specs=[
            rows(W_A + W_B, 0), rows(W_C, 0),
            rows(D_MODEL, 0), rows(D_MODEL, 1), rows(D_MODEL, 2),
            rows(D_MODEL, 0),
            full((W_A, D_MODEL)), full((W_B, D_MODEL)), full((W_C, D_MODEL)),
            full((D_MODEL, D_MODEL)), full((1, D_MODEL)),
        ],
        out_specs=[rows(D_MODEL, 0), rows(D_MODEL, 0)],
        out_shape=[jax.ShapeDtypeStruct((N_TOK, D_MODEL), F32),
                   jax.ShapeDtypeStruct((N_TOK, D_MODEL), BF16)],
        compiler_params=_cparams(("parallel",)),
        name="merge",
    )(y_ab, y_c, gates, gates, gates, x, lw["w_br_a"], lw["w_br_b"], lw["w_br_c"], lw["w_out"],
      lw["norm_ffn"])


def _router_kernel(x_ref, gn_ref, wr_ref, u_ref, route_ref, cnt_ref, cnt_scr):
    @pl.when(pl.program_id(0) == 0)
    def _():
        cnt_scr[...] = jnp.zeros_like(cnt_scr)

    h = _rmsnorm(x_ref[...], gn_ref[...])
    logits = jnp.dot(h, wr_ref[...], preferred_element_type=F32, precision=lax.Precision.HIGHEST)
    lane = lax.broadcasted_iota(jnp.int32, logits.shape, 1)
    neg = jnp.float32(-jnp.inf)
    logits = jnp.where(lane < N_EXPERTS, logits, neg)
    v1 = jnp.max(logits, axis=-1, keepdims=True)
    i1 = jnp.min(jnp.where(logits == v1, lane, LANES), axis=-1, keepdims=True)
    rest = jnp.where(lane == i1, neg, logits)
    v2 = jnp.max(rest, axis=-1, keepdims=True)
    i2 = jnp.min(jnp.where(rest == v2, lane, LANES), axis=-1, keepdims=True)
    e2 = jnp.exp(v2 - v1)
    den = 1.0 + e2
    oh1 = lane == i1
    oh2 = lane == i2
    m = jnp.where(oh1 | oh2, 1.0, 0.0)
    run = cnt_scr[0:1, :]
    befores = []
    for sb in range(ROUTER_TM // MOE_TM):
        m_sb = m[sb * MOE_TM:(sb + 1) * MOE_TM]
        befores.append(jnp.dot(u_ref[...], m_sb.astype(BF16), preferred_element_type=F32) + run)
        run = run + jnp.sum(m_sb, axis=0, keepdims=True)
    before = jnp.concatenate(befores, axis=0)
    r1 = jnp.sum(jnp.where(oh1, before, 0.0), axis=-1, keepdims=True)
    r2 = jnp.sum(jnp.where(oh2, before, 0.0), axis=-1, keepdims=True)
    cols = (1.0 / den, e2 / den, i1.astype(F32), i2.astype(F32), r1, r2)
    route = jnp.zeros(logits.shape, F32)
    for k, c in enumerate(cols):
        route = jnp.where(lane == k, c, route)
    route_ref[...] = route
    cnt = jnp.broadcast_to(run, cnt_scr.shape)
    cnt_scr[...] = cnt
    cnt_ref[...] = cnt


def _router(x, g, router_pad, u):
    tm = ROUTER_TM
    return pl.pallas_call(
        _router_kernel,
        grid=(N_TOK // tm,),
        in_specs=[
            pl.BlockSpec((tm, D_MODEL), lambda i: (i, 0)),
            pl.BlockSpec((1, D_MODEL), lambda i: (0, 0)),
            pl.BlockSpec((D_MODEL, LANES), lambda i: (0, 0)),
            pl.BlockSpec((MOE_TM, MOE_TM), lambda i: (0, 0)),
        ],
        out_specs=[pl.BlockSpec((tm, LANES), lambda i: (i, 0)),
                   pl.BlockSpec((SUBLANES, LANES), lambda i: (0, 0))],
        out_shape=[jax.ShapeDtypeStruct((N_TOK, LANES), F32),
                   jax.ShapeDtypeStruct((SUBLANES, LANES), F32)],
        scratch_shapes=[pltpu.VMEM((SUBLANES, LANES), F32)],
        compiler_params=_cparams(("arbitrary",)),
        name="router",
    )(x, g.reshape(1, D_MODEL), router_pad, u)


def _route_plan(route, cnt):
    t = MOE_T
    e12 = route[:, 2:4].astype(jnp.int32)
    rank = route[:, 4:6].astype(jnp.int32)
    counts = cnt[0, :N_EXPERTS].astype(jnp.int32)
    padded = ((counts + t - 1) // t) * t
    ends = jnp.cumsum(padded)
    pos = (ends - padded)[e12] + rank
    n_active = ends[-1] // t
    tile_start = jnp.arange(MOE_ROWS // t, dtype=jnp.int32) * t
    tile_expert = jnp.sum(tile_start[:, None] >= ends[None, :], axis=1).astype(jnp.int32)
    last_expert = jnp.sum((n_active - 1) * t >= ends).astype(jnp.int32)
    tile_expert = jnp.minimum(tile_expert, last_expert)
    pos_blk = pos.reshape(N_TOK // MOE_TM, MOE_TM, 2).transpose(0, 2, 1)
    return pos_blk, tile_expert, n_active.reshape(1).astype(jnp.int32)


def _row(ref, r):
    return ref.at[pl.ds(r, 1), :]


def _dispatch_kernel(pos_ref, x_ref, xs_in_hbm, xs_hbm, sem):
    del xs_in_hbm

    def issue(r, c):
        pltpu.make_async_copy(_row(x_ref, r), _row(xs_hbm, pos_ref[0, r]), sem).start()
        pltpu.make_async_copy(_row(x_ref, r), _row(xs_hbm, pos_ref[1, r]), sem).start()
        return c

    def drain(r, c):
        pltpu.make_async_copy(_row(x_ref, 0), _row(xs_hbm, 0), sem).wait()
        pltpu.make_async_copy(_row(x_ref, 0), _row(xs_hbm, 0), sem).wait()
        return c

    lax.fori_loop(0, MOE_TM, issue, 0, unroll=8)
    lax.fori_loop(0, MOE_TM, drain, 0, unroll=8)


def _dispatch(pos_blk, x):
    zeros = jnp.zeros((MOE_ROWS, D_MODEL), F32)
    return pl.pallas_call(
        _dispatch_kernel,
        grid=(N_TOK // MOE_TM,),
        in_specs=[
            pl.BlockSpec((None, 2, MOE_TM), lambda i: (i, 0, 0), memory_space=pltpu.SMEM),
            pl.BlockSpec((MOE_TM, D_MODEL), lambda i: (i, 0)),
            pl.BlockSpec(memory_space=pl.ANY),
        ],
        out_specs=pl.BlockSpec(memory_space=pl.ANY),
        out_shape=jax.ShapeDtypeStruct((MOE_ROWS, D_MODEL), F32),
        scratch_shapes=[pltpu.SemaphoreType.DMA(())],
        input_output_aliases={2: 0},
        compiler_params=_cparams(("arbitrary",)),
        name="moe_dispatch",
    )(pos_blk, x, zeros)


def _moe_ffn_kernel(te_ref, na_ref, xs_ref, gn_ref, w1_ref, w3_ref, w2_ref, ys_ref, hb_scr, acc_scr):
    del te_ref
    f = pl.program_id(1)
    active = pl.program_id(0) < na_ref[0]

    @pl.when(active & (f == 0))
    def _():
        hb_scr[...] = _rmsnorm(xs_ref[...], gn_ref[...]).astype(BF16)
        acc_scr[...] = jnp.zeros_like(acc_scr)

    @pl.when(active)
    def _():
        h = hb_scr[...]
        a = jnp.dot(h, w1_ref[...], preferred_element_type=F32)
        b = jnp.dot(h, w3_ref[...], preferred_element_type=F32)
        t = (a * jax.nn.sigmoid(a)) * b
        acc_scr[...] += jnp.dot(t.astype(BF16), w2_ref[...], preferred_element_type=F32)

    @pl.when(f == pl.num_programs(1) - 1)
    def _():
        ys_ref[...] = jnp.where(active, acc_scr[...], 0.0)


def _moe_ffn(tile_expert, n_active, xs, g, w1, w3, w2):
    t, tf = MOE_T, MOE_TF
    nf = w1.shape[2] // tf

    def wspec(shape, order):
        def index(r, f, te, na):
            ff = jnp.where(r < na[0], f, nf - 1)
            return (te[r], ff, 0) if order else (te[r], 0, ff)
        return pl.BlockSpec(shape, index)

    grid_spec = pltpu.PrefetchScalarGridSpec(
        num_scalar_prefetch=2,
        grid=(MOE_ROWS // t, nf),
        in_specs=[
            pl.BlockSpec((t, D_MODEL), lambda r, f, te, na: (r, 0)),
            pl.BlockSpec((1, D_MODEL), lambda r, f, te, na: (0, 0)),
            wspec((None, D_MODEL, tf), False),
            wspec((None, D_MODEL, tf), False),
            wspec((None, tf, D_MODEL), True),
        ],
        out_specs=pl.BlockSpec((t, D_MODEL), lambda r, f, te, na: (r, 0)),
        scratch_shapes=[pltpu.VMEM((t, D_MODEL), BF16), pltpu.VMEM((t, D_MODEL), F32)],
    )
    return pl.pallas_call(
        _moe_ffn_kernel,
        grid_spec=grid_spec,
        out_shape=jax.ShapeDtypeStruct((MOE_ROWS, D_MODEL), F32),
        compiler_params=_cparams(("parallel", "arbitrary")),
        name="moe_ffn",
    )(tile_expert, n_active, xs, g.reshape(1, D_MODEL), w1, w3, w2)


def _combine_kernel(pos_ref, route_ref, x_ref, gf_ref, ys_hbm, *rest, final_norm):
    out_refs, (y1_buf, y2_buf, sem) = rest[:-3], rest[-3:]
    def issue(r, c):
        pltpu.make_async_copy(_row(ys_hbm, pos_ref[0, r]), _row(y1_buf, r), sem).start()
        pltpu.make_async_copy(_row(ys_hbm, pos_ref[1, r]), _row(y2_buf, r), sem).start()
        return c

    def drain(r, c):
        pltpu.make_async_copy(_row(ys_hbm, 0), _row(y1_buf, 0), sem).wait()
        pltpu.make_async_copy(_row(ys_hbm, 0), _row(y2_buf, 0), sem).wait()
        return c

    lax.fori_loop(0, MOE_TM, issue, 0, unroll=8)
    lax.fori_loop(0, MOE_TM, drain, 0, unroll=8)
    y = x_ref[...] + (route_ref[:, 0:1] * y1_buf[...] + route_ref[:, 1:2] * y2_buf[...])
    if final_norm:
        y = _rmsnorm(y, gf_ref[...])
        is_prompt = pl.program_id(0) < N_PROMPT // MOE_TM

        @pl.when(is_prompt)
        def _():
            out_refs[0][...] = y

        @pl.when(jnp.logical_not(is_prompt))
        def _():
            out_refs[1][...] = y
    else:
        out_refs[0][...] = y


def _combine(pos_blk, route, x, g_final, ys, *, final_norm):
    tm = MOE_TM
    n_p = N_PROMPT // tm
    if final_norm:
        out_specs = [pl.BlockSpec((tm, D_MODEL), lambda i: (jnp.minimum(i, n_p - 1), 0)),
                     pl.BlockSpec((tm, D_MODEL), lambda i: (jnp.maximum(i - n_p, 0), 0))]
        out_shape = [jax.ShapeDtypeStruct((N_PROMPT, D_MODEL), F32),
                     jax.ShapeDtypeStruct((N_SAMPLE, D_MODEL), F32)]
    else:
        out_specs = pl.BlockSpec((tm, D_MODEL), lambda i: (i, 0))
        out_shape = jax.ShapeDtypeStruct((N_TOK, D_MODEL), F32)
    return pl.pallas_call(
        functools.partial(_combine_kernel, final_norm=final_norm),
        grid=(N_TOK // tm,),
        in_specs=[
            pl.BlockSpec((None, 2, tm), lambda i: (i, 0, 0), memory_space=pltpu.SMEM),
            pl.BlockSpec((tm, LANES), lambda i: (i, 0)),
            pl.BlockSpec((tm, D_MODEL), lambda i: (i, 0)),
            pl.BlockSpec((1, D_MODEL), lambda i: (0, 0)),
            pl.BlockSpec(memory_space=pl.ANY),
        ],
        out_specs=out_specs,
        out_shape=out_shape,
        scratch_shapes=[pltpu.VMEM((tm, D_MODEL), F32), pltpu.VMEM((tm, D_MODEL), F32),
                        pltpu.SemaphoreType.DMA(())],
        compiler_params=_cparams(("arbitrary",)),
        name="moe_combine",
    )(pos_blk, route, x, g_final.reshape(1, D_MODEL), ys)


def _ffn_kernel(h_ref, x_ref, w1_ref, w3_ref, w2_ref, gf_ref, o_ref, acc_scr, *, final_norm):
    f = pl.program_id(1)

    @pl.when(f == 0)
    def _():
        acc_scr[...] = jnp.zeros_like(acc_scr)

    h = h_ref[...]
    a = jnp.dot(h, w1_ref[...], preferred_element_type=F32)
    b = jnp.dot(h, w3_ref[...], preferred_element_type=F32)
    t = (a * jax.nn.sigmoid(a)) * b
    acc_scr[...] += jnp.dot(t.astype(BF16), w2_ref[...], preferred_element_type=F32)

    @pl.when(f == pl.num_programs(1) - 1)
    def _():
        y = x_ref[...] + acc_scr[...]
        if final_norm:
            y = _rmsnorm(y, gf_ref[...])
        o_ref[...] = y


def _ffn(h2, x, w1, w3, w2, g_final, *, final_norm):
    tm, tf = 1280, 256
    d_ff = w1.shape[1]
    return pl.pallas_call(
        functools.partial(_ffn_kernel, final_norm=final_norm),
        grid=(N_TOK // tm, d_ff // tf),
        in_specs=[
            pl.BlockSpec((tm, D_MODEL), lambda i, f: (i, 0)),
            pl.BlockSpec((tm, D_MODEL), lambda i, f: (i, 0)),
            pl.BlockSpec((D_MODEL, tf), lambda i, f: (0, f)),
            pl.BlockSpec((D_MODEL, tf), lambda i, f: (0, f)),
            pl.BlockSpec((tf, D_MODEL), lambda i, f: (f, 0)),
            pl.BlockSpec((1, D_MODEL), lambda i, f: (0, 0)),
        ],
        out_specs=pl.BlockSpec((tm, D_MODEL), lambda i, f: (i, 0)),
        out_shape=jax.ShapeDtypeStruct((N_TOK, D_MODEL), F32),
        scratch_shapes=[pltpu.VMEM((tm, D_MODEL), F32)],
        compiler_params=_cparams(("parallel", "arbitrary")),
        name="ffn_dense",
    )(h2, x, w1, w3, w2, g_final.reshape(1, D_MODEL))


def _pad_hist(hist, width):
    nseq, _, c = hist.shape
    return jnp.concatenate([jnp.zeros((nseq, SUBLANES - (width - 1), c), hist.dtype), hist], axis=1)


def _block_diag(w):
    h, d, _ = w.shape
    eye = jnp.eye(h, dtype=w.dtype)
    return (eye[:, None, :, None] * w[:, :, None, :]).reshape(h * d, h * d)


def kernel(x_prompt, x_sample, cache_k, cache_v, state_conv_a, state_conv_b, state_h, page_table,
           norm_mix, w_in, conv_a_w, conv_b_w, conv_b_bias, rg_w_r, rg_b_r, rg_w_i, rg_b_i,
           rg_lambda, sb_bias, w_br_a, w_br_b, w_br_c, w_out, norm_ffn, ffn_w1, ffn_w3, ffn_w2,
           moe_router, moe_w1, moe_w3, moe_w2, norm_final):
    depth = w_in.shape[0]
    n_phys = cache_k.shape[1]
    cache_k4 = cache_k.transpose(0, 1, 3, 4, 2).reshape(depth, n_phys, W_C, PAGE_SIZE)
    cache_v4 = cache_v.transpose(0, 1, 3, 4, 2).reshape(depth, n_phys, W_C, PAGE_SIZE)
    x = jnp.concatenate([x_prompt.reshape(N_PROMPT, D_MODEL), x_sample.reshape(N_SAMPLE, D_MODEL)])
    ri = lax.broadcasted_iota(jnp.int32, (ATT_TK, ATT_TK), 0)
    ci = lax.broadcasted_iota(jnp.int32, (ATT_TK, ATT_TK), 1)
    u = (ri > ci).astype(BF16)
    zeros8 = jnp.zeros((BATCH, SUBLANES, W_A), F32)

    st_p, st_s = [], []
    for l in range(depth):
        lw = {
            "conv_a_w": conv_a_w[l], "conv_b_w": conv_b_w[l], "conv_b_bias": conv_b_bias[l][None],
            "rg_w_r": _block_diag(rg_w_r[l]).astype(BF16), "rg_b_r": rg_b_r[l].reshape(1, W_B),
            "rg_w_i": _block_diag(rg_w_i[l]).astype(BF16), "rg_b_i": rg_b_i[l].reshape(1, W_B),
            "rg_lambda": rg_lambda[l][None],
            "w_br_a": w_br_a[l].astype(BF16), "w_br_b": w_br_b[l].astype(BF16),
            "w_br_c": w_br_c[l].astype(BF16), "w_out": w_out[l].astype(BF16),
            "norm_ffn": norm_ffn[l][None],
        }
        proj = _in_proj(x, norm_mix[l], w_in[l][:, :COL_G].astype(BF16), COL_G // 2, F32)
        gates = _in_proj(x, norm_mix[l], w_in[l][:, COL_G:].astype(BF16), (IN_WIDTH - COL_G) // 2, BF16)

        y_ab, sa_p, sb_p, hl_p = _mixers(proj, zeros8, zeros8, zeros8, lw, jnp.zeros((N_TOK, W_C), F32),
                                         nseq=BATCH, t=SEQ, row0=0, first_pos_zero=True)
        h0_8 = jnp.broadcast_to(state_h[l][:, None, :], (DEC_BATCH, SUBLANES, W_B))
        y_ab, sa_s, sb_s, hl_s = _mixers(proj, _pad_hist(state_conv_a[l], CONV_A),
                                         _pad_hist(state_conv_b[l], CONV_B), h0_8, lw, y_ab,
                                         nseq=DEC_BATCH, t=DEC_SEQ, row0=N_PROMPT,
                                         first_pos_zero=False)

        y_c, kt, vt = _attn_prompt(proj, sb_bias[l], u, jnp.zeros((N_TOK, W_C), F32))
        y_c = _attn_sample(proj, y_c, cache_k4, cache_v4, page_table, l, sb_bias[l], u)

        x, h2 = _merge(y_ab, y_c, gates, x, lw)

        last = l == depth - 1
        if l % 2 == 0:
            i = l // 2
            x = _ffn(h2, x, ffn_w1[i].astype(BF16), ffn_w3[i].astype(BF16), ffn_w2[i].astype(BF16),
                     norm_final, final_norm=last)
        else:
            i = l // 2
            router_pad = jnp.pad(moe_router[i], ((0, 0), (0, LANES - N_EXPERTS)))
            route, cnt = _router(x, norm_ffn[l], router_pad, u)
            pos_blk, tile_expert, n_active = _route_plan(route, cnt)
            xs = _dispatch(pos_blk, x)
            ys = _moe_ffn(tile_expert, n_active, xs, norm_ffn[l], moe_w1[i].astype(BF16),
                          moe_w3[i].astype(BF16), moe_w2[i].astype(BF16))
            x = _combine(pos_blk, route, x, norm_final, ys, final_norm=last)
            if last:
                y_prompt, y_sample = x

        kp = kt.reshape(BATCH, H_C, DH_C, SEQ).transpose(0, 3, 1, 2)
        vp = vt.reshape(BATCH, H_C, DH_C, SEQ).transpose(0, 3, 1, 2)
        ks = proj[N_PROMPT:, COL_K:COL_K + W_C].reshape(DEC_BATCH, DEC_SEQ, H_C, DH_C)
        vs = proj[N_PROMPT:, COL_V:COL_V + W_C].reshape(DEC_BATCH, DEC_SEQ, H_C, DH_C)
        st_p.append((kp, vp, sa_p[:, SUBLANES - (CONV_A - 1):], sb_p[:, SUBLANES - (CONV_B - 1):],
                     hl_p[:, 0]))
        st_s.append((ks, vs, sa_s[:, SUBLANES - (CONV_A - 1):], sb_s[:, SUBLANES - (CONV_B - 1):],
                     hl_s[:, 0]))

    if depth % 2 == 1:
        y_prompt, y_sample = x[:N_PROMPT], x[N_PROMPT:]
    y_prompt = y_prompt.reshape(BATCH, SEQ, D_MODEL)
    y_sample = y_sample.reshape(DEC_BATCH, DEC_SEQ, D_MODEL)
    stack = lambda sts, k: jnp.stack([s[k] for s in sts])
    return (y_prompt, y_sample,
            stack(st_p, 0), stack(st_p, 1), stack(st_p, 2), stack(st_p, 3), stack(st_p, 4),
            stack(st_s, 0), stack(st_s, 1), stack(st_s, 2), stack(st_s, 3), stack(st_s, 4))
```

```python
import functools

import jax
import jax.numpy as jnp
from jax import lax
from jax.experimental import pallas as pl
from jax.experimental.pallas import tpu as pltpu

F32 = jnp.float32
BF16 = jnp.bfloat16

D_MODEL = 1024
BATCH = 4
SEQ = 4096
DEC_BATCH = 32
DEC_SEQ = 8
PAGE_SIZE = 128
W_A = 256
W_B = 256
H_B = 4
DH_B = 64
CONV_A = 3
CONV_B = 4
RG_C = 8.0
H_C = 8
DH_C = 64
W_C = 512
IN_WIDTH = 5632
N_EXPERTS = 8
EPS = 1e-6
LOG2E = 1.4426950408889634

N_PROMPT = BATCH * SEQ
N_SAMPLE = DEC_BATCH * DEC_SEQ
N_TOK = N_PROMPT + N_SAMPLE

COL_Q = 1024
COL_K = 1536
COL_V = 2048
COL_G = 2560

SUBLANES = 8
LANES = 128
VMEM_LIMIT = 56 * 1024 * 1024

ATT_TQ = 512
ATT_TK = 256
PAGES_PER_STEP = 32

MOE_TM = 256
ROUTER_TM = 1280
MOE_T = 512
MOE_TF = 1792
MOE_ROWS = -(-(2 * N_TOK + N_EXPERTS * (MOE_T - 1)) // MOE_T) * MOE_T


def _cparams(sem):
    return pltpu.CompilerParams(dimension_semantics=sem, vmem_limit_bytes=VMEM_LIMIT)


def _rmsnorm(x, g):
    y = x * lax.rsqrt(jnp.mean(x * x, axis=-1, keepdims=True) + EPS)
    return y * g


def _inproj_kernel(x_ref, g_ref, w_ref, o_ref, h_scr):
    @pl.when(pl.program_id(1) == 0)
    def _():
        h_scr[...] = _rmsnorm(x_ref[...], g_ref[...]).astype(BF16)

    o_ref[...] = jnp.dot(h_scr[...], w_ref[...], preferred_element_type=F32).astype(o_ref.dtype)


def _in_proj(x, g, w_bf16, tn, out_dtype):
    tm = 1280
    width = w_bf16.shape[1]
    return pl.pallas_call(
        _inproj_kernel,
        grid=(N_TOK // tm, width // tn),
        in_specs=[
            pl.BlockSpec((tm, D_MODEL), lambda i, j: (i, 0)),
            pl.BlockSpec((1, D_MODEL), lambda i, j: (0, 0)),
            pl.BlockSpec((D_MODEL, tn), lambda i, j: (0, j)),
        ],
        out_specs=pl.BlockSpec((tm, tn), lambda i, j: (i, j)),
        out_shape=jax.ShapeDtypeStruct((N_TOK, width), out_dtype),
        scratch_shapes=[pltpu.VMEM((tm, D_MODEL), BF16)],
        compiler_params=_cparams(("parallel", "arbitrary")),
        name="in_proj",
    )(x, g.reshape(1, D_MODEL), w_bf16)


def _softplus(x):
    return jnp.maximum(x, 0.0) + jnp.log1p(jnp.exp(-jnp.abs(x)))


def _mixers_kernel(p_ref, ha_ref, hb_ref, h0_ref, wa_ref, wb_ref, bb_ref, wr_ref, br_ref,
                   wi_ref, bi_ref, lam_ref, y_prev_hbm, y_ref, sa_ref, sb_ref, hl_ref, ua_buf, xb_buf,
                   hc_scr, *, tt, first_pos_zero):
    del y_prev_hbm
    ti = pl.program_id(1)

    @pl.when(ti == 0)
    def _():
        ua_buf[0:SUBLANES, :] = ha_ref[...]
        xb_buf[0:SUBLANES, :] = hb_ref[...]
        hc_scr[...] = h0_ref[...]

    bg = p_ref[:, 0:W_A]
    cg = p_ref[:, W_A:2 * W_A]
    xa = p_ref[:, 2 * W_A:3 * W_A]
    xb = p_ref[:, 3 * W_A:3 * W_A + W_B]

    ua = cg * xa
    ua_buf[SUBLANES:SUBLANES + tt, :] = ua
    ya = ua_buf[pl.ds(SUBLANES - 2, tt), :] * wa_ref[0:1, :]
    ya = ya + ua_buf[pl.ds(SUBLANES - 1, tt), :] * wa_ref[1:2, :]
    ya = ya + ua * wa_ref[2:3, :]
    y_ref[:, 0:W_A] = bg * ya

    xb_buf[SUBLANES:SUBLANES + tt, :] = xb
    xc = xb_buf[pl.ds(SUBLANES - 3, tt), :] * wb_ref[0:1, :]
    xc = xc + xb_buf[pl.ds(SUBLANES - 2, tt), :] * wb_ref[1:2, :]
    xc = xc + xb_buf[pl.ds(SUBLANES - 1, tt), :] * wb_ref[2:3, :]
    xc = xc + xb * wb_ref[3:4, :]
    xc = xc + bb_ref[...]
    xcb = xc.astype(BF16)
    r = jax.nn.sigmoid(jnp.dot(xcb, wr_ref[...], preferred_element_type=F32) + br_ref[...])
    ig = jax.nn.sigmoid(jnp.dot(xcb, wi_ref[...], preferred_element_type=F32) + bi_ref[...])
    log_a = (-RG_C * r) * _softplus(-lam_ref[...])
    a = jnp.exp(log_a)
    mult = jnp.sqrt(1.0 - jnp.exp(2.0 * log_a))
    row = lax.broadcasted_iota(jnp.int32, (tt, W_B), 0)
    if first_pos_zero:
        mult = jnp.where((row == 0) & (ti == 0), 1.0, mult)
    b = mult * (ig * xc)

    s = 1
    while s < tt:
        keep = row >= s
        a_sh = pltpu.roll(a, s, 0)
        b_sh = pltpu.roll(b, s, 0)
        b = jnp.where(keep, a * b_sh + b, b)
        a = jnp.where(keep, a * a_sh, a)
        s *= 2
    h = a * hc_scr[0:1, :] + b
    y_ref[:, W_A:W_A + W_B] = h

    last8_a = ua_buf[tt:tt + SUBLANES, :]
    last8_b = xb_buf[tt:tt + SUBLANES, :]
    hlast = jnp.broadcast_to(h[tt - 1:tt, :], (SUBLANES, W_B))
    ua_buf[0:SUBLANES, :] = last8_a
    xb_buf[0:SUBLANES, :] = last8_b
    hc_scr[...] = hlast
    sa_ref[...] = last8_a
    sb_ref[...] = last8_b
    hl_ref[...] = hlast


def _mixers(proj, hist_a8, hist_b8, h0_8, lw, y_prev, *, nseq, t, row0, first_pos_zero):
    tt = min(t, 512)
    nt = t // tt
    rb0 = row0 // tt
    kern = functools.partial(_mixers_kernel, tt=tt, first_pos_zero=first_pos_zero)
    seq3 = pl.BlockSpec((None, SUBLANES, W_A), lambda s, i: (s, 0, 0))
    full = lambda shape: pl.BlockSpec(shape, lambda s, i: (0,) * len(shape))
    in_specs = [
        pl.BlockSpec((tt, 4 * W_A), lambda s, i: (rb0 + s * nt + i, 0)),
        seq3, seq3, seq3,
        full((CONV_A, W_A)), full((CONV_B, W_B)), full((1, W_B)),
        full((W_B, W_B)), full((1, W_B)), full((W_B, W_B)), full((1, W_B)), full((1, W_B)),
        pl.BlockSpec(memory_space=pl.ANY),
    ]
    args = [proj, hist_a8, hist_b8, h0_8, lw["conv_a_w"], lw["conv_b_w"], lw["conv_b_bias"],
            lw["rg_w_r"], lw["rg_b_r"], lw["rg_w_i"], lw["rg_b_i"], lw["rg_lambda"], y_prev]
    st = jax.ShapeDtypeStruct((nseq, SUBLANES, W_A), F32)
    return pl.pallas_call(
        kern,
        grid=(nseq, nt),
        in_specs=in_specs,
        out_specs=[
            pl.BlockSpec((tt, W_A + W_B), lambda s, i: (rb0 + s * nt + i, 0)),
            seq3, seq3, seq3,
        ],
        out_shape=[jax.ShapeDtypeStruct((N_TOK, W_A + W_B), F32), st, st, st],
        scratch_shapes=[
            pltpu.VMEM((SUBLANES + tt, W_A), F32),
            pltpu.VMEM((SUBLANES + tt, W_B), F32),
            pltpu.VMEM((SUBLANES, W_B), F32),
        ],
        input_output_aliases={len(args) - 1: 0},
        compiler_params=_cparams(("parallel", "arbitrary")),
        name="mixers_t%d" % t,
    )(*args)


def _log2_sigmoid_pair(z2):
    m = jnp.minimum(z2, 0.0)
    n = m - z2
    l2 = jnp.log2(1.0 + jnp.exp2(m + n))
    return m - l2, n - l2


def _sb_chunk(qm, bias, kc, vc, u, carry, mask=None):
    z2 = lax.dot_general(qm, kc, (((1,), (1,)), ((), ())), preferred_element_type=F32) + bias
    ls, lk = _log2_sigmoid_pair(z2)
    if mask is not None:
        lk = jnp.where(mask, lk, 0.0)
    after = jnp.dot(lk.astype(BF16), u, preferred_element_type=F32)
    total = after[:, 0:1] + lk[:, 0:1]
    w = jnp.exp2(ls + after + carry)
    if mask is not None:
        w = jnp.where(mask, w, 0.0)
    return jnp.dot(w.astype(BF16), vc, preferred_element_type=F32), total


def _sb_diagonal(qm, bias, k1, v1, k0, v0, u):
    c = u.shape[0]
    row = lax.broadcasted_iota(jnp.int32, (c, c), 0)
    col = lax.broadcasted_iota(jnp.int32, (c, c), 1)
    causal = col < row
    q_up, q_lo = qm[:c], qm[c:]
    pv_l1, t_l1 = _sb_chunk(q_lo, bias, k1, v1, u, 0.0, causal)
    pv_l0, t_l0 = _sb_chunk(q_lo, bias, k0, v0, u, t_l1)
    pv_u, t_u = _sb_chunk(q_up, bias, k0, v0, u, 0.0, causal)
    return (jnp.concatenate([pv_u, pv_l1 + pv_l0], axis=0),
            jnp.concatenate([t_u, t_l1 + t_l0], axis=0))


def _attn_prompt_kernel(bias_ref, q_ref, k_ref, v_ref, u_ref, y_prev_hbm, *rest, n_prev):
    del y_prev_hbm
    if n_prev:
        ktp_ref, vtp_ref = rest[:2]
        rest = rest[2:]
    o_ref, kt_ref, vt_ref, kb_scr, vb_scr, acc_scr, cr_scr = rest
    hp = pl.program_id(1)
    qi = pl.program_id(2)
    tq, tk = ATT_TQ, ATT_TK
    nblk = tq // tk

    @pl.when(qi == 0)
    def _():
        kb_scr[...] = k_ref[...].astype(BF16)
        vb_scr[...] = v_ref[...].astype(BF16)
        if n_prev:
            kt_ref[0:n_prev] = ktp_ref[...]
            vt_ref[0:n_prev] = vtp_ref[...]
        for c in range(SEQ // tq):
            rows = slice(c * tq, (c + 1) * tq)
            kt_ref[n_prev, :, rows] = k_ref[rows, :].T
            vt_ref[n_prev, :, rows] = v_ref[rows, :].T

    q = q_ref[...] * (LOG2E * DH_C ** -0.5)
    lane = lax.broadcasted_iota(jnp.int32, (tq, LANES), 1)
    qm = [jnp.where(lane < DH_C, q, 0.0).astype(BF16), jnp.where(lane >= DH_C, q, 0.0).astype(BF16)]
    bias = [bias_ref[2 * hp] * LOG2E, bias_ref[2 * hp + 1] * LOG2E]
    u = u_ref[...]

    st1 = pl.multiple_of((qi * nblk + 1) * tk, tk)
    st0 = pl.multiple_of(qi * nblk * tk, tk)
    for hh in range(2):
        acc, carry = _sb_diagonal(qm[hh], bias[hh], kb_scr[pl.ds(st1, tk), :], vb_scr[pl.ds(st1, tk), :],
                                  kb_scr[pl.ds(st0, tk), :], vb_scr[pl.ds(st0, tk), :], u)
        acc_scr[hh] = acc
        cr_scr[hh] = jnp.broadcast_to(carry, (tq, LANES))

    def group(blk, hh, n):
        carry = cr_scr[hh, :, 0:1]
        pv_sum, t_sum = None, None
        for i in range(n):
            st = pl.multiple_of((blk - i) * tk, tk)
            pv, t = _sb_chunk(qm[hh], bias[hh], kb_scr[pl.ds(st, tk), :], vb_scr[pl.ds(st, tk), :], u,
                              carry if t_sum is None else carry + t_sum)
            pv_sum = pv if pv_sum is None else pv_sum + pv
            t_sum = t if t_sum is None else t_sum + t
        acc_scr[hh] += pv_sum
        cr_scr[hh] += jnp.broadcast_to(t_sum, (tq, LANES))

    def body(it, _):
        for hh in range(2):
            group(qi * nblk - 1 - 2 * nblk * it, hh, 2 * nblk)
        return 0

    lax.fori_loop(0, qi // 2, body, 0)

    @pl.when(qi % 2 == 1)
    def _():
        for hh in range(2):
            group(nblk - 1, hh, nblk)

    o_ref[...] = jnp.where(lane < DH_C, acc_scr[0], acc_scr[1])


def _attn_prompt(proj, sb_bias, u, y_prev, kv_prev):
    tq = ATT_TQ
    nq = SEQ // tq
    n_prev = 0 if kv_prev is None else kv_prev[0].shape[0]
    kv_spec = lambda n: pl.BlockSpec((n, None, LANES, SEQ), lambda b, hp, qi, s: (0, b, hp, 0))
    grid_spec = pltpu.PrefetchScalarGridSpec(
        num_scalar_prefetch=1,
        grid=(BATCH, H_C // 2, nq),
        in_specs=[
            pl.BlockSpec((tq, LANES), lambda b, hp, qi, s: (b * nq + qi, COL_Q // LANES + hp)),
            pl.BlockSpec((SEQ, LANES), lambda b, hp, qi, s: (b, COL_K // LANES + hp)),
            pl.BlockSpec((SEQ, LANES), lambda b, hp, qi, s: (b, COL_V // LANES + hp)),
            pl.BlockSpec((ATT_TK, ATT_TK), lambda b, hp, qi, s: (0, 0)),
            pl.BlockSpec(memory_space=pl.ANY),
        ] + ([kv_spec(n_prev)] * 2 if n_prev else []),
        out_specs=[
            pl.BlockSpec((tq, LANES), lambda b, hp, qi, s: (b * nq + qi, hp)),
            kv_spec(n_prev + 1), kv_spec(n_prev + 1),
        ],
        scratch_shapes=[
            pltpu.VMEM((SEQ, LANES), BF16),
            pltpu.VMEM((SEQ, LANES), BF16),
            pltpu.VMEM((2, tq, LANES), F32),
            pltpu.VMEM((2, tq, LANES), F32),
        ],
    )
    kv_t = jax.ShapeDtypeStruct((n_prev + 1, BATCH, W_C, SEQ), F32)
    return pl.pallas_call(
        functools.partial(_attn_prompt_kernel, n_prev=n_prev),
        grid_spec=grid_spec,
        out_shape=[jax.ShapeDtypeStruct((N_TOK, W_C), F32), kv_t, kv_t],
        input_output_aliases={5: 0},
        compiler_params=_cparams(("parallel", "parallel", "arbitrary")),
        name="attn_prompt",
    )(sb_bias, proj, proj, proj, u, y_prev, *(kv_prev or ()))


def _attn_sample_kernel(pt_ref, bias_ref, q_ref, kn_ref, vn_ref, u_ref, *rest):
    npg = PAGES_PER_STEP
    k_refs = rest[:npg]
    v_refs = rest[npg:2 * npg]
    (o_ref, qbd_scr, bias_scr, kn_scr, vn_scr, kcat_scr, vcat_scr, acc_scr,
     cr_scr) = rest[2 * npg + 1:]
    j = pl.program_id(1)
    nrow = H_C * DEC_SEQ
    u = u_ref[0:PAGE_SIZE, 0:PAGE_SIZE]

    @pl.when(j == 0)
    def _():
        row = lax.broadcasted_iota(jnp.int32, (nrow, W_C), 0)
        lane = lax.broadcasted_iota(jnp.int32, (nrow, W_C), 1)
        q8 = q_ref[...] * (LOG2E * DH_C ** -0.5)
        qt = jnp.concatenate([q8] * H_C, axis=0)
        qbd_scr[...] = jnp.where(row // DEC_SEQ == lane // DH_C, qt, 0.0).astype(BF16)
        rowb = lax.broadcasted_iota(jnp.int32, (nrow, LANES), 0)
        bias = jnp.zeros((nrow, LANES), F32)
        for h in range(H_C):
            bias = jnp.where(rowb // DEC_SEQ == h, bias_ref[h] * LOG2E, bias)
        bias_scr[...] = bias
        kn_scr[...] = jnp.zeros((PAGE_SIZE, W_C), BF16)
        vn_scr[...] = jnp.zeros((PAGE_SIZE, W_C), BF16)
        kn_scr[0:DEC_SEQ, :] = kn_ref[...].astype(BF16)
        vn_scr[0:DEC_SEQ, :] = vn_ref[...].astype(BF16)
        rowc = lax.broadcasted_iota(jnp.int32, (nrow, PAGE_SIZE), 0)
        colc = lax.broadcasted_iota(jnp.int32, (nrow, PAGE_SIZE), 1)
        pv, total = _sb_chunk(qbd_scr[...], bias_scr[:, 0:1], kn_scr[...], vn_scr[...], u, 0.0,
                              mask=colc < rowc % DEC_SEQ)
        acc_scr[...] = pv
        cr_scr[...] = jnp.broadcast_to(total, (nrow, LANES))

    for p in range(npg):
        kcat_scr[:, p * PAGE_SIZE:(p + 1) * PAGE_SIZE] = k_refs[p][...].astype(BF16)
        vcat_scr[:, p * PAGE_SIZE:(p + 1) * PAGE_SIZE] = v_refs[p][...].astype(BF16)
    z2 = jnp.dot(qbd_scr[...], kcat_scr[...], preferred_element_type=F32)
    ls, lk = _log2_sigmoid_pair(z2 + bias_scr[:, 0:1])
    lkb = lk.astype(BF16)
    carry = cr_scr[:, 0:1]
    afters = []
    for p in range(npg):
        sl = slice(p * PAGE_SIZE, (p + 1) * PAGE_SIZE)
        a = jnp.dot(lkb[:, sl], u, preferred_element_type=F32)
        afters.append(a + carry)
        carry = carry + (a[:, 0:1] + lk[:, p * PAGE_SIZE:p * PAGE_SIZE + 1])
    w = jnp.exp2(ls + jnp.concatenate(afters, axis=1)).astype(BF16)
    acc_scr[...] += lax.dot_general(w, vcat_scr[...], (((1,), (1,)), ((), ())),
                                    preferred_element_type=F32)
    cr_scr[...] = jnp.broadcast_to(carry, (nrow, LANES))

    @pl.when(j == pl.num_programs(1) - 1)
    def _():
        lane = lax.broadcasted_iota(jnp.int32, (DEC_SEQ, W_C), 1)
        out = jnp.zeros((DEC_SEQ, W_C), F32)
        for h in range(H_C):
            out = out + jnp.where(lane // DH_C == h, acc_scr[h * DEC_SEQ:(h + 1) * DEC_SEQ, :], 0.0)
        o_ref[...] = out


def _attn_sample(proj, y_c, cache_k4, cache_v4, page_table, layer, sb_bias, u):
    npg = PAGES_PER_STEP
    n_pages = page_table.shape[1]
    nsteps = n_pages // npg
    rb0 = N_PROMPT // DEC_SEQ
    nrow = H_C * DEC_SEQ

    def row_spec(colblk):
        return pl.BlockSpec((DEC_SEQ, W_C), lambda b, j, pt, s: (rb0 + b, colblk))

    def page_spec(p):
        return pl.BlockSpec((None, None, W_C, PAGE_SIZE),
                            lambda b, j, pt, s: (layer, pt[b, n_pages - 1 - (j * npg + p)], 0, 0))

    grid_spec = pltpu.PrefetchScalarGridSpec(
        num_scalar_prefetch=2,
        grid=(DEC_BATCH, nsteps),
        in_specs=([row_spec(COL_Q // W_C), row_spec(COL_K // W_C), row_spec(COL_V // W_C),
                   pl.BlockSpec((ATT_TK, ATT_TK), lambda b, j, pt, s: (0, 0))]
                  + [page_spec(p) for p in range(npg)]
                  + [page_spec(p) for p in range(npg)]
                  + [pl.BlockSpec(memory_space=pl.ANY)]),
        out_specs=pl.BlockSpec((DEC_SEQ, W_C), lambda b, j, pt, s: (rb0 + b, 0)),
        scratch_shapes=[
            pltpu.VMEM((nrow, W_C), BF16),
            pltpu.VMEM((nrow, LANES), F32),
            pltpu.VMEM((PAGE_SIZE, W_C), BF16),
            pltpu.VMEM((PAGE_SIZE, W_C), BF16),
            pltpu.VMEM((W_C, npg * PAGE_SIZE), BF16),
            pltpu.VMEM((W_C, npg * PAGE_SIZE), BF16),
            pltpu.VMEM((nrow, W_C), F32),
            pltpu.VMEM((nrow, LANES), F32),
        ],
    )
    n_in = 2 + 4 + 2 * npg
    return pl.pallas_call(
        _attn_sample_kernel,
        grid_spec=grid_spec,
        out_shape=jax.ShapeDtypeStruct((N_TOK, W_C), F32),
        input_output_aliases={n_in: 0},
        compiler_params=_cparams(("parallel", "arbitrary")),
        name="attn_sample",
    )(page_table, sb_bias, proj, proj, proj, u, *([cache_k4] * npg), *([cache_v4] * npg), y_c)


def _merge_kernel(yab_ref, yc_ref, ga_ref, gb_ref, gc_ref, x_ref, wa_ref, wb_ref, wc_ref, wo_ref,
                  gn_ref, xo_ref, h2_ref):
    ya = yab_ref[:, 0:W_A].astype(BF16)
    yb = yab_ref[:, W_A:W_A + W_B].astype(BF16)
    yc = yc_ref[...].astype(BF16)
    gate = lambda ref: jax.nn.sigmoid(ref[...].astype(F32))
    m = gate(ga_ref) * jnp.dot(ya, wa_ref[...], preferred_element_type=F32)
    m = m + gate(gb_ref) * jnp.dot(yb, wb_ref[...], preferred_element_type=F32)
    m = m + gate(gc_ref) * jnp.dot(yc, wc_ref[...], preferred_element_type=F32)
    xn = x_ref[...] + jnp.dot(m.astype(BF16), wo_ref[...], preferred_element_type=F32)
    xo_ref[...] = xn
    h2_ref[...] = _rmsnorm(xn, gn_ref[...]).astype(BF16)


def _merge(y_ab, y_c, gates, x, lw):
    tm = 640
    rows = lambda w, c: pl.BlockSpec((tm, w), lambda i: (i, c))
    full = lambda shape: pl.BlockSpec(shape, lambda i: (0, 0))
    return pl.pallas_call(
        _merge_kernel,
        grid=(N_TOK // tm,),
        in_specs=[
            rows(W_A + W_B, 0), rows(W_C, 0),
            rows(D_MODEL, 0), rows(D_MODEL, 1), rows(D_MODEL, 2),
            rows(D_MODEL, 0),
            full((W_A, D_MODEL)), full((W_B, D_MODEL)), full((W_C, D_MODEL)),
            full((D_MODEL, D_MODEL)), full((1, D_MODEL)),
        ],
        out_specs=[rows(D_MODEL, 0), rows(D_MODEL, 0)],
        out_shape=[jax.ShapeDtypeStruct((N_TOK, D_MODEL), F32),
                   jax.ShapeDtypeStruct((N_TOK, D_MODEL), BF16)],
        compiler_params=_cparams(("parallel",)),
        name="merge",
    )(y_ab, y_c, gates, gates, gates, x, lw["w_br_a"], lw["w_br_b"], lw["w_br_c"], lw["w_out"],
      lw["norm_ffn"])


def _router_kernel(x_ref, gn_ref, wr_ref, u_ref, route_ref, cnt_ref, cnt_scr):
    @pl.when(pl.program_id(0) == 0)
    def _():
        cnt_scr[...] = jnp.zeros_like(cnt_scr)

    h = _rmsnorm(x_ref[...], gn_ref[...])
    logits = jnp.dot(h, wr_ref[...], preferred_element_type=F32, precision=lax.Precision.HIGHEST)
    lane = lax.broadcasted_iota(jnp.int32, logits.shape, 1)
    neg = jnp.float32(-jnp.inf)
    logits = jnp.where(lane < N_EXPERTS, logits, neg)
    v1 = jnp.max(logits, axis=-1, keepdims=True)
    i1 = jnp.min(jnp.where(logits == v1, lane, LANES), axis=-1, keepdims=True)
    rest = jnp.where(lane == i1, neg, logits)
    v2 = jnp.max(rest, axis=-1, keepdims=True)
    i2 = jnp.min(jnp.where(rest == v2, lane, LANES), axis=-1, keepdims=True)
    e2 = jnp.exp(v2 - v1)
    den = 1.0 + e2
    oh1 = lane == i1
    oh2 = lane == i2
    m = jnp.where(oh1 | oh2, 1.0, 0.0)
    run = cnt_scr[0:1, :]
    befores = []
    for sb in range(ROUTER_TM // MOE_TM):
        m_sb = m[sb * MOE_TM:(sb + 1) * MOE_TM]
        befores.append(jnp.dot(u_ref[...], m_sb.astype(BF16), preferred_element_type=F32) + run)
        run = run + jnp.sum(m_sb, axis=0, keepdims=True)
    before = jnp.concatenate(befores, axis=0)
    r1 = jnp.sum(jnp.where(oh1, before, 0.0), axis=-1, keepdims=True)
    r2 = jnp.sum(jnp.where(oh2, before, 0.0), axis=-1, keepdims=True)
    cols = (1.0 / den, e2 / den, i1.astype(F32), i2.astype(F32), r1, r2)
    route = jnp.zeros(logits.shape, F32)
    for k, c in enumerate(cols):
        route = jnp.where(lane == k, c, route)
    route_ref[...] = route
    cnt = jnp.broadcast_to(run, cnt_scr.shape)
    cnt_scr[...] = cnt
    cnt_ref[...] = cnt


def _router(x, g, router_pad, u):
    tm = ROUTER_TM
    return pl.pallas_call(
        _router_kernel,
        grid=(N_TOK // tm,),
        in_specs=[
            pl.BlockSpec((tm, D_MODEL), lambda i: (i, 0)),
            pl.BlockSpec((1, D_MODEL), lambda i: (0, 0)),
            pl.BlockSpec((D_MODEL, LANES), lambda i: (0, 0)),
            pl.BlockSpec((MOE_TM, MOE_TM), lambda i: (0, 0)),
        ],
        out_specs=[pl.BlockSpec((tm, LANES), lambda i: (i, 0)),
                   pl.BlockSpec((SUBLANES, LANES), lambda i: (0, 0))],
        out_shape=[jax.ShapeDtypeStruct((N_TOK, LANES), F32),
                   jax.ShapeDtypeStruct((SUBLANES, LANES), F32)],
        scratch_shapes=[pltpu.VMEM((SUBLANES, LANES), F32)],
        compiler_params=_cparams(("arbitrary",)),
        name="router",
    )(x, g.reshape(1, D_MODEL), router_pad, u)


def _route_plan(route, cnt):
    t = MOE_T
    e12 = route[:, 2:4].astype(jnp.int32)
    rank = route[:, 4:6].astype(jnp.int32)
    counts = cnt[0, :N_EXPERTS].astype(jnp.int32)
    padded = ((counts + t - 1) // t) * t
    ends = jnp.cumsum(padded)
    pos = (ends - padded)[e12] + rank
    n_active = ends[-1] // t
    tile_start = jnp.arange(MOE_ROWS // t, dtype=jnp.int32) * t
    tile_expert = jnp.sum(tile_start[:, None] >= ends[None, :], axis=1).astype(jnp.int32)
    last_expert = jnp.sum((n_active - 1) * t >= ends).astype(jnp.int32)
    tile_expert = jnp.minimum(tile_expert, last_expert)
    pos_blk = pos.reshape(N_TOK // MOE_TM, MOE_TM, 2).transpose(0, 2, 1)
    return pos_blk, tile_expert, n_active.reshape(1).astype(jnp.int32)


def _row(ref, r):
    return ref.at[pl.ds(r, 1), :]


def _dispatch_kernel(pos_ref, x_ref, xs_in_hbm, xs_hbm, sem):
    del xs_in_hbm

    def issue(r, c):
        pltpu.make_async_copy(_row(x_ref, r), _row(xs_hbm, pos_ref[0, r]), sem).start()
        pltpu.make_async_copy(_row(x_ref, r), _row(xs_hbm, pos_ref[1, r]), sem).start()
        return c

    def drain(r, c):
        pltpu.make_async_copy(_row(x_ref, 0), _row(xs_hbm, 0), sem).wait()
        pltpu.make_async_copy(_row(x_ref, 0), _row(xs_hbm, 0), sem).wait()
        return c

    lax.fori_loop(0, MOE_TM, issue, 0, unroll=8)
    lax.fori_loop(0, MOE_TM, drain, 0, unroll=8)


def _dispatch(pos_blk, x):
    zeros = jnp.zeros((MOE_ROWS, D_MODEL), F32)
    return pl.pallas_call(
        _dispatch_kernel,
        grid=(N_TOK // MOE_TM,),
        in_specs=[
            pl.BlockSpec((None, 2, MOE_TM), lambda i: (i, 0, 0), memory_space=pltpu.SMEM),
            pl.BlockSpec((MOE_TM, D_MODEL), lambda i: (i, 0)),
            pl.BlockSpec(memory_space=pl.ANY),
        ],
        out_specs=pl.BlockSpec(memory_space=pl.ANY),
        out_shape=jax.ShapeDtypeStruct((MOE_ROWS, D_MODEL), F32),
        scratch_shapes=[pltpu.SemaphoreType.DMA(())],
        input_output_aliases={2: 0},
        compiler_params=_cparams(("arbitrary",)),
        name="moe_dispatch",
    )(pos_blk, x, zeros)


def _moe_ffn_kernel(te_ref, na_ref, xs_ref, gn_ref, w1_ref, w3_ref, w2_ref, ys_ref, hb_scr, acc_scr):
    del te_ref
    f = pl.program_id(1)
    active = pl.program_id(0) < na_ref[0]

    @pl.when(active & (f == 0))
    def _():
        hb_scr[...] = _rmsnorm(xs_ref[...], gn_ref[...]).astype(BF16)
        acc_scr[...] = jnp.zeros_like(acc_scr)

    @pl.when(active)
    def _():
        h = hb_scr[...]
        a = jnp.dot(h, w1_ref[...], preferred_element_type=F32)
        b = jnp.dot(h, w3_ref[...], preferred_element_type=F32)
        t = (a * jax.nn.sigmoid(a)) * b
        acc_scr[...] += jnp.dot(t.astype(BF16), w2_ref[...], preferred_element_type=F32)

    @pl.when(f == pl.num_programs(1) - 1)
    def _():
        ys_ref[...] = jnp.where(active, acc_scr[...], 0.0)


def _moe_ffn(tile_expert, n_active, xs, g, w1, w3, w2):
    t, tf = MOE_T, MOE_TF
    nf = w1.shape[2] // tf

    def wspec(shape, order):
        def index(r, f, te, na):
            ff = jnp.where(r < na[0], f, nf - 1)
            return (te[r], ff, 0) if order else (te[r], 0, ff)
        return pl.BlockSpec(shape, index)

    grid_spec = pltpu.PrefetchScalarGridSpec(
        num_scalar_prefetch=2,
        grid=(MOE_ROWS // t, nf),
        in_specs=[
            pl.BlockSpec((t, D_MODEL), lambda r, f, te, na: (r, 0)),
            pl.BlockSpec((1, D_MODEL), lambda r, f, te, na: (0, 0)),
            wspec((None, D_MODEL, tf), False),
            wspec((None, D_MODEL, tf), False),
            wspec((None, tf, D_MODEL), True),
        ],
        out_specs=pl.BlockSpec((t, D_MODEL), lambda r, f, te, na: (r, 0)),
        scratch_shapes=[pltpu.VMEM((t, D_MODEL), BF16), pltpu.VMEM((t, D_MODEL), F32)],
    )
    return pl.pallas_call(
        _moe_ffn_kernel,
        grid_spec=grid_spec,
        out_shape=jax.ShapeDtypeStruct((MOE_ROWS, D_MODEL), F32),
        compiler_params=_cparams(("parallel", "arbitrary")),
        name="moe_ffn",
    )(tile_expert, n_active, xs, g.reshape(1, D_MODEL), w1, w3, w2)


def _combine_kernel(pos_ref, route_ref, x_ref, gf_ref, ys_hbm, *rest, final_norm):
    out_refs, (y1_buf, y2_buf, sem) = rest[:-3], rest[-3:]
    def issue(r, c):
        pltpu.make_async_copy(_row(ys_hbm, pos_ref[0, r]), _row(y1_buf, r), sem).start()
        pltpu.make_async_copy(_row(ys_hbm, pos_ref[1, r]), _row(y2_buf, r), sem).start()
        return c

    def drain(r, c):
        pltpu.make_async_copy(_row(ys_hbm, 0), _row(y1_buf, 0), sem).wait()
        pltpu.make_async_copy(_row(ys_hbm, 0), _row(y2_buf, 0), sem).wait()
        return c

    lax.fori_loop(0, MOE_TM, issue, 0, unroll=8)
    lax.fori_loop(0, MOE_TM, drain, 0, unroll=8)
    y = x_ref[...] + (route_ref[:, 0:1] * y1_buf[...] + route_ref[:, 1:2] * y2_buf[...])
    if final_norm:
        y = _rmsnorm(y, gf_ref[...])
        is_prompt = pl.program_id(0) < N_PROMPT // MOE_TM

        @pl.when(is_prompt)
        def _():
            out_refs[0][...] = y

        @pl.when(jnp.logical_not(is_prompt))
        def _():
            out_refs[1][...] = y
    else:
        out_refs[0][...] = y


def _combine(pos_blk, route, x, g_final, ys, *, final_norm):
    tm = MOE_TM
    n_p = N_PROMPT // tm
    if final_norm:
        out_specs = [pl.BlockSpec((tm, D_MODEL), lambda i: (jnp.minimum(i, n_p - 1), 0)),
                     pl.BlockSpec((tm, D_MODEL), lambda i: (jnp.maximum(i - n_p, 0), 0))]
        out_shape = [jax.ShapeDtypeStruct((N_PROMPT, D_MODEL), F32),
                     jax.ShapeDtypeStruct((N_SAMPLE, D_MODEL), F32)]
    else:
        out_specs = pl.BlockSpec((tm, D_MODEL), lambda i: (i, 0))
        out_shape = jax.ShapeDtypeStruct((N_TOK, D_MODEL), F32)
    return pl.pallas_call(
        functools.partial(_combine_kernel, final_norm=final_norm),
        grid=(N_TOK // tm,),
        in_specs=[
            pl.BlockSpec((None, 2, tm), lambda i: (i, 0, 0), memory_space=pltpu.SMEM),
            pl.BlockSpec((tm, LANES), lambda i: (i, 0)),
            pl.BlockSpec((tm, D_MODEL), lambda i: (i, 0)),
            pl.BlockSpec((1, D_MODEL), lambda i: (0, 0)),
            pl.BlockSpec(memory_space=pl.ANY),
        ],
        out_specs=out_specs,
        out_shape=out_shape,
        scratch_shapes=[pltpu.VMEM((tm, D_MODEL), F32), pltpu.VMEM((tm, D_MODEL), F32),
                        pltpu.SemaphoreType.DMA(())],
        compiler_params=_cparams(("arbitrary",)),
        name="moe_combine",
    )(pos_blk, route, x, g_final.reshape(1, D_MODEL), ys)


def _ffn_kernel(h_ref, x_ref, w1_ref, w3_ref, w2_ref, gf_ref, o_ref, acc_scr, *, final_norm):
    f = pl.program_id(1)

    @pl.when(f == 0)
    def _():
        acc_scr[...] = jnp.zeros_like(acc_scr)

    h = h_ref[...]
    a = jnp.dot(h, w1_ref[...], preferred_element_type=F32)
    b = jnp.dot(h, w3_ref[...], preferred_element_type=F32)
    t = (a * jax.nn.sigmoid(a)) * b
    acc_scr[...] += jnp.dot(t.astype(BF16), w2_ref[...], preferred_element_type=F32)

    @pl.when(f == pl.num_programs(1) - 1)
    def _():
        y = x_ref[...] + acc_scr[...]
        if final_norm:
            y = _rmsnorm(y, gf_ref[...])
        o_ref[...] = y


def _ffn(h2, x, w1, w3, w2, g_final, *, final_norm):
    tm, tf = 1280, 256
    d_ff = w1.shape[1]
    return pl.pallas_call(
        functools.partial(_ffn_kernel, final_norm=final_norm),
        grid=(N_TOK // tm, d_ff // tf),
        in_specs=[
            pl.BlockSpec((tm, D_MODEL), lambda i, f: (i, 0)),
            pl.BlockSpec((tm, D_MODEL), lambda i, f: (i, 0)),
            pl.BlockSpec((D_MODEL, tf), lambda i, f: (0, f)),
            pl.BlockSpec((D_MODEL, tf), lambda i, f: (0, f)),
            pl.BlockSpec((tf, D_MODEL), lambda i, f: (f, 0)),
            pl.BlockSpec((1, D_MODEL), lambda i, f: (0, 0)),
        ],
        out_specs=pl.BlockSpec((tm, D_MODEL), lambda i, f: (i, 0)),
        out_shape=jax.ShapeDtypeStruct((N_TOK, D_MODEL), F32),
        scratch_shapes=[pltpu.VMEM((tm, D_MODEL), F32)],
        compiler_params=_cparams(("parallel", "arbitrary")),
        name="ffn_dense",
    )(h2, x, w1, w3, w2, g_final.reshape(1, D_MODEL))


def _pad_hist(hist, width):
    nseq, _, c = hist.shape
    return jnp.concatenate([jnp.zeros((nseq, SUBLANES - (width - 1), c), hist.dtype), hist], axis=1)


def _block_diag(w):
    h, d, _ = w.shape
    eye = jnp.eye(h, dtype=w.dtype)
    return (eye[:, None, :, None] * w[:, :, None, :]).reshape(h * d, h * d)


def kernel(x_prompt, x_sample, cache_k, cache_v, state_conv_a, state_conv_b, state_h, page_table,
           norm_mix, w_in, conv_a_w, conv_b_w, conv_b_bias, rg_w_r, rg_b_r, rg_w_i, rg_b_i,
           rg_lambda, sb_bias, w_br_a, w_br_b, w_br_c, w_out, norm_ffn, ffn_w1, ffn_w3, ffn_w2,
           moe_router, moe_w1, moe_w3, moe_w2, norm_final):
    depth = w_in.shape[0]
    n_phys = cache_k.shape[1]
    cache_k4 = cache_k.transpose(0, 1, 3, 4, 2).reshape(depth, n_phys, W_C, PAGE_SIZE)
    cache_v4 = cache_v.transpose(0, 1, 3, 4, 2).reshape(depth, n_phys, W_C, PAGE_SIZE)
    x = jnp.concatenate([x_prompt.reshape(N_PROMPT, D_MODEL), x_sample.reshape(N_SAMPLE, D_MODEL)])
    ri = lax.broadcasted_iota(jnp.int32, (ATT_TK, ATT_TK), 0)
    ci = lax.broadcasted_iota(jnp.int32, (ATT_TK, ATT_TK), 1)
    u = (ri > ci).astype(BF16)
    zeros8 = jnp.zeros((BATCH, SUBLANES, W_A), F32)

    st_p, st_s = [], []
    kv_prev = None
    for l in range(depth):
        lw = {
            "conv_a_w": conv_a_w[l], "conv_b_w": conv_b_w[l], "conv_b_bias": conv_b_bias[l][None],
            "rg_w_r": _block_diag(rg_w_r[l]).astype(BF16), "rg_b_r": rg_b_r[l].reshape(1, W_B),
            "rg_w_i": _block_diag(rg_w_i[l]).astype(BF16), "rg_b_i": rg_b_i[l].reshape(1, W_B),
            "rg_lambda": rg_lambda[l][None],
            "w_br_a": w_br_a[l].astype(BF16), "w_br_b": w_br_b[l].astype(BF16),
            "w_br_c": w_br_c[l].astype(BF16), "w_out": w_out[l].astype(BF16),
            "norm_ffn": norm_ffn[l][None],
        }
        proj = _in_proj(x, norm_mix[l], w_in[l][:, :COL_G].astype(BF16), COL_G // 2, F32)
        gates = _in_proj(x, norm_mix[l], w_in[l][:, COL_G:].astype(BF16), (IN_WIDTH - COL_G) // 2, BF16)

        y_ab, sa_p, sb_p, hl_p = _mixers(proj, zeros8, zeros8, zeros8, lw, jnp.zeros((N_TOK, W_C), F32),
                                         nseq=BATCH, t=SEQ, row0=0, first_pos_zero=True)
        h0_8 = jnp.broadcast_to(state_h[l][:, None, :], (DEC_BATCH, SUBLANES, W_B))
        y_ab, sa_s, sb_s, hl_s = _mixers(proj, _pad_hist(state_conv_a[l], CONV_A),
                                         _pad_hist(state_conv_b[l], CONV_B), h0_8, lw, y_ab,
                                         nseq=DEC_BATCH, t=DEC_SEQ, row0=N_PROMPT,
                                         first_pos_zero=False)

        y_c, kt, vt = _attn_prompt(proj, sb_bias[l], u, jnp.zeros((N_TOK, W_C), F32), kv_prev)
        kv_prev = (kt, vt)
        y_c = _attn_sample(proj, y_c, cache_k4, cache_v4, page_table, l, sb_bias[l], u)

        x, h2 = _merge(y_ab, y_c, gates, x, lw)

        last = l == depth - 1
        if l % 2 == 0:
            i = l // 2
            x = _ffn(h2, x, ffn_w1[i].astype(BF16), ffn_w3[i].astype(BF16), ffn_w2[i].astype(BF16),
                     norm_final, final_norm=last)
        else:
            i = l // 2
            router_pad = jnp.pad(moe_router[i], ((0, 0), (0, LANES - N_EXPERTS)))
            route, cnt = _router(x, norm_ffn[l], router_pad, u)
            pos_blk, tile_expert, n_active = _route_plan(route, cnt)
            xs = _dispatch(pos_blk, x)
            ys = _moe_ffn(tile_expert, n_active, xs, norm_ffn[l], moe_w1[i].astype(BF16),
                          moe_w3[i].astype(BF16), moe_w2[i].astype(BF16))
            x = _combine(pos_blk, route, x, norm_final, ys, final_norm=last)
            if last:
                y_prompt, y_sample = x

        ks = proj[N_PROMPT:, COL_K:COL_K + W_C].reshape(DEC_BATCH, DEC_SEQ, H_C, DH_C)
        vs = proj[N_PROMPT:, COL_V:COL_V + W_C].reshape(DEC_BATCH, DEC_SEQ, H_C, DH_C)
        st_p.append((None, None, sa_p[:, SUBLANES - (CONV_A - 1):], sb_p[:, SUBLANES - (CONV_B - 1):],
                     hl_p[:, 0]))
        st_s.append((ks, vs, sa_s[:, SUBLANES - (CONV_A - 1):], sb_s[:, SUBLANES - (CONV_B - 1):],
                     hl_s[:, 0]))

    if depth % 2 == 1:
        y_prompt, y_sample = x[:N_PROMPT], x[N_PROMPT:]
    y_prompt = y_prompt.reshape(BATCH, SEQ, D_MODEL)
    y_sample = y_sample.reshape(DEC_BATCH, DEC_SEQ, D_MODEL)
    stack = lambda sts, k: jnp.stack([s[k] for s in sts])
    k_prompt, v_prompt = (a.reshape(depth, BATCH, H_C, DH_C, SEQ).transpose(0, 1, 4, 2, 3) for a in kv_prev)
    return (y_prompt, y_sample,
            k_prompt, v_prompt, stack(st_p, 2), stack(st_p, 3), stack(st_p, 4),
            stack(st_s, 0), stack(st_s, 1), stack(st_s, 2), stack(st_s, 3), stack(st_s, 4))
```

```python
import functools

import jax
import jax.numpy as jnp
from jax import lax
from jax.experimental import pallas as pl
from jax.experimental.pallas import tpu as pltpu

F32 = jnp.float32
BF16 = jnp.bfloat16

D_MODEL = 1024
BATCH = 4
SEQ = 4096
DEC_BATCH = 32
DEC_SEQ = 8
PAGE_SIZE = 128
W_A = 256
W_B = 256
H_B = 4
DH_B = 64
CONV_A = 3
CONV_B = 4
RG_C = 8.0
H_C = 8
DH_C = 64
W_C = 512
IN_WIDTH = 5632
N_EXPERTS = 8
EPS = 1e-6
LOG2E = 1.4426950408889634

N_PROMPT = BATCH * SEQ
N_SAMPLE = DEC_BATCH * DEC_SEQ
N_TOK = N_PROMPT + N_SAMPLE

COL_Q = 1024
COL_K = 1536
COL_V = 2048
COL_G = 2560

SUBLANES = 8
LANES = 128
VMEM_LIMIT = 56 * 1024 * 1024

ATT_TQ = 512
ATT_TK = 256
PAGES_PER_STEP = 32

MOE_TM = 256
ROUTER_TM = 1280
MOE_T = 512
MOE_TF = 1792
MOE_ROWS = -(-(2 * N_TOK + N_EXPERTS * (MOE_T - 1)) // MOE_T) * MOE_T


def _cparams(sem):
    return pltpu.CompilerParams(dimension_semantics=sem, vmem_limit_bytes=VMEM_LIMIT)


def _rmsnorm(x, g):
    y = x * lax.rsqrt(jnp.mean(x * x, axis=-1, keepdims=True) + EPS)
    return y * g


def _inproj_kernel(x_ref, g_ref, w_ref, o_ref, h_scr):
    @pl.when(pl.program_id(1) == 0)
    def _():
        h_scr[...] = _rmsnorm(x_ref[...], g_ref[...]).astype(BF16)

    o_ref[...] = jnp.dot(h_scr[...], w_ref[...], preferred_element_type=F32).astype(o_ref.dtype)


def _in_proj(x, g, w_bf16, tn, out_dtype):
    tm = 1280
    width = w_bf16.shape[1]
    return pl.pallas_call(
        _inproj_kernel,
        grid=(N_TOK // tm, width // tn),
        in_specs=[
            pl.BlockSpec((tm, D_MODEL), lambda i, j: (i, 0)),
            pl.BlockSpec((1, D_MODEL), lambda i, j: (0, 0)),
            pl.BlockSpec((D_MODEL, tn), lambda i, j: (0, j)),
        ],
        out_specs=pl.BlockSpec((tm, tn), lambda i, j: (i, j)),
        out_shape=jax.ShapeDtypeStruct((N_TOK, width), out_dtype),
        scratch_shapes=[pltpu.VMEM((tm, D_MODEL), BF16)],
        compiler_params=_cparams(("parallel", "arbitrary")),
        name="in_proj",
    )(x, g.reshape(1, D_MODEL), w_bf16)


def _softplus(x):
    return jnp.maximum(x, 0.0) + jnp.log1p(jnp.exp(-jnp.abs(x)))


def _mixers_kernel(p_ref, ha_ref, hb_ref, h0_ref, wa_ref, wb_ref, bb_ref, wr_ref, br_ref,
                   wi_ref, bi_ref, lam_ref, y_prev_hbm, y_ref, sa_ref, sb_ref, hl_ref, ua_buf, xb_buf,
                   hc_scr, *, tt, first_pos_zero):
    del y_prev_hbm
    ti = pl.program_id(1)

    @pl.when(ti == 0)
    def _():
        ua_buf[0:SUBLANES, :] = ha_ref[...]
        xb_buf[0:SUBLANES, :] = hb_ref[...]
        hc_scr[...] = h0_ref[...]

    bg = p_ref[:, 0:W_A]
    cg = p_ref[:, W_A:2 * W_A]
    xa = p_ref[:, 2 * W_A:3 * W_A]
    xb = p_ref[:, 3 * W_A:3 * W_A + W_B]

    ua = cg * xa
    ua_buf[SUBLANES:SUBLANES + tt, :] = ua
    ya = ua_buf[pl.ds(SUBLANES - 2, tt), :] * wa_ref[0:1, :]
    ya = ya + ua_buf[pl.ds(SUBLANES - 1, tt), :] * wa_ref[1:2, :]
    ya = ya + ua * wa_ref[2:3, :]
    y_ref[:, 0:W_A] = bg * ya

    xb_buf[SUBLANES:SUBLANES + tt, :] = xb
    xc = xb_buf[pl.ds(SUBLANES - 3, tt), :] * wb_ref[0:1, :]
    xc = xc + xb_buf[pl.ds(SUBLANES - 2, tt), :] * wb_ref[1:2, :]
    xc = xc + xb_buf[pl.ds(SUBLANES - 1, tt), :] * wb_ref[2:3, :]
    xc = xc + xb * wb_ref[3:4, :]
    xc = xc + bb_ref[...]
    xcb = xc.astype(BF16)
    r = jax.nn.sigmoid(jnp.dot(xcb, wr_ref[...], preferred_element_type=F32) + br_ref[...])
    ig = jax.nn.sigmoid(jnp.dot(xcb, wi_ref[...], preferred_element_type=F32) + bi_ref[...])
    log_a = (-RG_C * r) * _softplus(-lam_ref[...])
    a = jnp.exp(log_a)
    mult = jnp.sqrt(1.0 - jnp.exp(2.0 * log_a))
    row = lax.broadcasted_iota(jnp.int32, (tt, W_B), 0)
    if first_pos_zero:
        mult = jnp.where((row == 0) & (ti == 0), 1.0, mult)
    b = mult * (ig * xc)

    s = 1
    while s < tt:
        keep = row >= s
        a_sh = pltpu.roll(a, s, 0)
        b_sh = pltpu.roll(b, s, 0)
        b = jnp.where(keep, a * b_sh + b, b)
        a = jnp.where(keep, a * a_sh, a)
        s *= 2
    h = a * hc_scr[0:1, :] + b
    y_ref[:, W_A:W_A + W_B] = h

    last8_a = ua_buf[tt:tt + SUBLANES, :]
    last8_b = xb_buf[tt:tt + SUBLANES, :]
    hlast = jnp.broadcast_to(h[tt - 1:tt, :], (SUBLANES, W_B))
    ua_buf[0:SUBLANES, :] = last8_a
    xb_buf[0:SUBLANES, :] = last8_b
    hc_scr[...] = hlast
    sa_ref[...] = last8_a
    sb_ref[...] = last8_b
    hl_ref[...] = hlast


def _mixers(proj, hist_a8, hist_b8, h0_8, lw, y_prev, *, nseq, t, row0, first_pos_zero):
    tt = min(t, 1024)
    nt = t // tt
    rb0 = row0 // tt
    kern = functools.partial(_mixers_kernel, tt=tt, first_pos_zero=first_pos_zero)
    seq3 = pl.BlockSpec((None, SUBLANES, W_A), lambda s, i: (s, 0, 0))
    full = lambda shape: pl.BlockSpec(shape, lambda s, i: (0,) * len(shape))
    in_specs = [
        pl.BlockSpec((tt, 4 * W_A), lambda s, i: (rb0 + s * nt + i, 0)),
        seq3, seq3, seq3,
        full((CONV_A, W_A)), full((CONV_B, W_B)), full((1, W_B)),
        full((W_B, W_B)), full((1, W_B)), full((W_B, W_B)), full((1, W_B)), full((1, W_B)),
        pl.BlockSpec(memory_space=pl.ANY),
    ]
    args = [proj, hist_a8, hist_b8, h0_8, lw["conv_a_w"], lw["conv_b_w"], lw["conv_b_bias"],
            lw["rg_w_r"], lw["rg_b_r"], lw["rg_w_i"], lw["rg_b_i"], lw["rg_lambda"], y_prev]
    st = jax.ShapeDtypeStruct((nseq, SUBLANES, W_A), F32)
    return pl.pallas_call(
        kern,
        grid=(nseq, nt),
        in_specs=in_specs,
        out_specs=[
            pl.BlockSpec((tt, W_A + W_B), lambda s, i: (rb0 + s * nt + i, 0)),
            seq3, seq3, seq3,
        ],
        out_shape=[jax.ShapeDtypeStruct((N_TOK, W_A + W_B), F32), st, st, st],
        scratch_shapes=[
            pltpu.VMEM((SUBLANES + tt, W_A), F32),
            pltpu.VMEM((SUBLANES + tt, W_B), F32),
            pltpu.VMEM((SUBLANES, W_B), F32),
        ],
        input_output_aliases={len(args) - 1: 0},
        compiler_params=_cparams(("parallel", "arbitrary")),
        name="mixers_t%d" % t,
    )(*args)


def _log2_sigmoid_pair(z2):
    m = jnp.minimum(z2, 0.0)
    n = m - z2
    l2 = jnp.log2(1.0 + jnp.exp2(m + n))
    return m - l2, n - l2


def _sb_chunk(qm, bias, kc, vc, u, carry, mask=None):
    z2 = lax.dot_general(qm, kc, (((1,), (1,)), ((), ())), preferred_element_type=F32) + bias
    ls, lk = _log2_sigmoid_pair(z2)
    if mask is not None:
        lk = jnp.where(mask, lk, 0.0)
    after = jnp.dot(lk.astype(BF16), u, preferred_element_type=F32)
    total = after[:, 0:1] + lk[:, 0:1]
    w = jnp.exp2(ls + after + carry)
    if mask is not None:
        w = jnp.where(mask, w, 0.0)
    return jnp.dot(w.astype(BF16), vc, preferred_element_type=F32), total


def _sb_diagonal(qm, bias, k1, v1, k0, v0, u):
    c = u.shape[0]
    row = lax.broadcasted_iota(jnp.int32, (c, c), 0)
    col = lax.broadcasted_iota(jnp.int32, (c, c), 1)
    causal = col < row
    q_up, q_lo = qm[:c], qm[c:]
    pv_l1, t_l1 = _sb_chunk(q_lo, bias, k1, v1, u, 0.0, causal)
    pv_l0, t_l0 = _sb_chunk(q_lo, bias, k0, v0, u, t_l1)
    pv_u, t_u = _sb_chunk(q_up, bias, k0, v0, u, 0.0, causal)
    return (jnp.concatenate([pv_u, pv_l1 + pv_l0], axis=0),
            jnp.concatenate([t_u, t_l1 + t_l0], axis=0))


def _attn_prompt_kernel(bias_ref, q_ref, k_ref, v_ref, u_ref, y_prev_hbm, *rest, n_prev):
    del y_prev_hbm
    if n_prev:
        ktp_ref, vtp_ref = rest[:2]
        rest = rest[2:]
    o_ref, kt_ref, vt_ref, kb_scr, vb_scr, acc_scr, cr_scr = rest
    hp = pl.program_id(1)
    qi = pl.program_id(2)
    tq, tk = ATT_TQ, ATT_TK
    nblk = tq // tk

    @pl.when(qi == 0)
    def _():
        kb_scr[...] = k_ref[...].astype(BF16)
        vb_scr[...] = v_ref[...].astype(BF16)
        if n_prev:
            kt_ref[0:n_prev] = ktp_ref[...]
            vt_ref[0:n_prev] = vtp_ref[...]
        for c in range(SEQ // tq):
            rows = slice(c * tq, (c + 1) * tq)
            kt_ref[n_prev, :, rows] = k_ref[rows, :].T
            vt_ref[n_prev, :, rows] = v_ref[rows, :].T

    q = q_ref[...] * (LOG2E * DH_C ** -0.5)
    lane = lax.broadcasted_iota(jnp.int32, (tq, LANES), 1)
    qm = [jnp.where(lane < DH_C, q, 0.0).astype(BF16), jnp.where(lane >= DH_C, q, 0.0).astype(BF16)]
    bias = [bias_ref[2 * hp] * LOG2E, bias_ref[2 * hp + 1] * LOG2E]
    u = u_ref[...]

    st1 = pl.multiple_of((qi * nblk + 1) * tk, tk)
    st0 = pl.multiple_of(qi * nblk * tk, tk)
    for hh in range(2):
        acc, carry = _sb_diagonal(qm[hh], bias[hh], kb_scr[pl.ds(st1, tk), :], vb_scr[pl.ds(st1, tk), :],
                                  kb_scr[pl.ds(st0, tk), :], vb_scr[pl.ds(st0, tk), :], u)
        acc_scr[hh] = acc
        cr_scr[hh] = jnp.broadcast_to(carry, (tq, LANES))

    def group(blk, hh, n):
        carry = cr_scr[hh, :, 0:1]
        pv_sum, t_sum = None, None
        for i in range(n):
            st = pl.multiple_of((blk - i) * tk, tk)
            pv, t = _sb_chunk(qm[hh], bias[hh], kb_scr[pl.ds(st, tk), :], vb_scr[pl.ds(st, tk), :], u,
                              carry if t_sum is None else carry + t_sum)
            pv_sum = pv if pv_sum is None else pv_sum + pv
            t_sum = t if t_sum is None else t_sum + t
        acc_scr[hh] += pv_sum
        cr_scr[hh] += jnp.broadcast_to(t_sum, (tq, LANES))

    def body(it, _):
        for hh in range(2):
            group(qi * nblk - 1 - 2 * nblk * it, hh, 2 * nblk)
        return 0

    lax.fori_loop(0, qi // 2, body, 0)

    @pl.when(qi % 2 == 1)
    def _():
        for hh in range(2):
            group(nblk - 1, hh, nblk)

    o_ref[...] = jnp.where(lane < DH_C, acc_scr[0], acc_scr[1])


def _attn_prompt(proj, sb_bias, u, y_prev, kv_prev):
    tq = ATT_TQ
    nq = SEQ // tq
    n_prev = 0 if kv_prev is None else kv_prev[0].shape[0]
    kv_spec = lambda n: pl.BlockSpec((n, None, LANES, SEQ), lambda b, hp, qi, s: (0, b, hp, 0))
    grid_spec = pltpu.PrefetchScalarGridSpec(
        num_scalar_prefetch=1,
        grid=(BATCH, H_C // 2, nq),
        in_specs=[
            pl.BlockSpec((tq, LANES), lambda b, hp, qi, s: (b * nq + qi, COL_Q // LANES + hp)),
            pl.BlockSpec((SEQ, LANES), lambda b, hp, qi, s: (b, COL_K // LANES + hp)),
            pl.BlockSpec((SEQ, LANES), lambda b, hp, qi, s: (b, COL_V // LANES + hp)),
            pl.BlockSpec((ATT_TK, ATT_TK), lambda b, hp, qi, s: (0, 0)),
            pl.BlockSpec(memory_space=pl.ANY),
        ] + ([kv_spec(n_prev)] * 2 if n_prev else []),
        out_specs=[
            pl.BlockSpec((tq, LANES), lambda b, hp, qi, s: (b * nq + qi, hp)),
            kv_spec(n_prev + 1), kv_spec(n_prev + 1),
        ],
        scratch_shapes=[
            pltpu.VMEM((SEQ, LANES), BF16),
            pltpu.VMEM((SEQ, LANES), BF16),
            pltpu.VMEM((2, tq, LANES), F32),
            pltpu.VMEM((2, tq, LANES), F32),
        ],
    )
    kv_t = jax.ShapeDtypeStruct((n_prev + 1, BATCH, W_C, SEQ), F32)
    return pl.pallas_call(
        functools.partial(_attn_prompt_kernel, n_prev=n_prev),
        grid_spec=grid_spec,
        out_shape=[jax.ShapeDtypeStruct((N_TOK, W_C), F32), kv_t, kv_t],
        input_output_aliases={5: 0},
        compiler_params=_cparams(("parallel", "parallel", "arbitrary")),
        name="attn_prompt",
    )(sb_bias, proj, proj, proj, u, y_prev, *(kv_prev or ()))


def _attn_sample_kernel(pt_ref, bias_ref, q_ref, kn_ref, vn_ref, u_ref, *rest):
    npg = PAGES_PER_STEP
    k_refs = rest[:npg]
    v_refs = rest[npg:2 * npg]
    (o_ref, qbd_scr, bias_scr, kn_scr, vn_scr, kcat_scr, vcat_scr, acc_scr,
     cr_scr) = rest[2 * npg + 1:]
    j = pl.program_id(1)
    nrow = H_C * DEC_SEQ
    u = u_ref[0:PAGE_SIZE, 0:PAGE_SIZE]

    @pl.when(j == 0)
    def _():
        row = lax.broadcasted_iota(jnp.int32, (nrow, W_C), 0)
        lane = lax.broadcasted_iota(jnp.int32, (nrow, W_C), 1)
        q8 = q_ref[...] * (LOG2E * DH_C ** -0.5)
        qt = jnp.concatenate([q8] * H_C, axis=0)
        qbd_scr[...] = jnp.where(row // DEC_SEQ == lane // DH_C, qt, 0.0).astype(BF16)
        rowb = lax.broadcasted_iota(jnp.int32, (nrow, LANES), 0)
        bias = jnp.zeros((nrow, LANES), F32)
        for h in range(H_C):
            bias = jnp.where(rowb // DEC_SEQ == h, bias_ref[h] * LOG2E, bias)
        bias_scr[...] = bias
        kn_scr[...] = jnp.zeros((PAGE_SIZE, W_C), BF16)
        vn_scr[...] = jnp.zeros((PAGE_SIZE, W_C), BF16)
        kn_scr[0:DEC_SEQ, :] = kn_ref[...].astype(BF16)
        vn_scr[0:DEC_SEQ, :] = vn_ref[...].astype(BF16)
        rowc = lax.broadcasted_iota(jnp.int32, (nrow, PAGE_SIZE), 0)
        colc = lax.broadcasted_iota(jnp.int32, (nrow, PAGE_SIZE), 1)
        pv, total = _sb_chunk(qbd_scr[...], bias_scr[:, 0:1], kn_scr[...], vn_scr[...], u, 0.0,
                              mask=colc < rowc % DEC_SEQ)
        acc_scr[...] = pv
        cr_scr[...] = jnp.broadcast_to(total, (nrow, LANES))

    for p in range(npg):
        kcat_scr[:, p * PAGE_SIZE:(p + 1) * PAGE_SIZE] = k_refs[p][...].astype(BF16)
        vcat_scr[:, p * PAGE_SIZE:(p + 1) * PAGE_SIZE] = v_refs[p][...].astype(BF16)
    z2 = jnp.dot(qbd_scr[...], kcat_scr[...], preferred_element_type=F32)
    ls, lk = _log2_sigmoid_pair(z2 + bias_scr[:, 0:1])
    lkb = lk.astype(BF16)
    carry = cr_scr[:, 0:1]
    afters = []
    for p in range(npg):
        sl = slice(p * PAGE_SIZE, (p + 1) * PAGE_SIZE)
        a = jnp.dot(lkb[:, sl], u, preferred_element_type=F32)
        afters.append(a + carry)
        carry = carry + (a[:, 0:1] + lk[:, p * PAGE_SIZE:p * PAGE_SIZE + 1])
    w = jnp.exp2(ls + jnp.concatenate(afters, axis=1)).astype(BF16)
    acc_scr[...] += lax.dot_general(w, vcat_scr[...], (((1,), (1,)), ((), ())),
                                    preferred_element_type=F32)
    cr_scr[...] = jnp.broadcast_to(carry, (nrow, LANES))

    @pl.when(j == pl.num_programs(1) - 1)
    def _():
        lane = lax.broadcasted_iota(jnp.int32, (DEC_SEQ, W_C), 1)
        out = jnp.zeros((DEC_SEQ, W_C), F32)
        for h in range(H_C):
            out = out + jnp.where(lane // DH_C == h, acc_scr[h * DEC_SEQ:(h + 1) * DEC_SEQ, :], 0.0)
        o_ref[...] = out


def _attn_sample(proj, y_c, cache_k4, cache_v4, page_table, layer, sb_bias, u):
    npg = PAGES_PER_STEP
    n_pages = page_table.shape[1]
    nsteps = n_pages // npg
    rb0 = N_PROMPT // DEC_SEQ
    nrow = H_C * DEC_SEQ

    def row_spec(colblk):
        return pl.BlockSpec((DEC_SEQ, W_C), lambda b, j, pt, s: (rb0 + b, colblk))

    def page_spec(p):
        return pl.BlockSpec((None, None, W_C, PAGE_SIZE),
                            lambda b, j, pt, s: (layer, pt[b, n_pages - 1 - (j * npg + p)], 0, 0))

    grid_spec = pltpu.PrefetchScalarGridSpec(
        num_scalar_prefetch=2,
        grid=(DEC_BATCH, nsteps),
        in_specs=([row_spec(COL_Q // W_C), row_spec(COL_K // W_C), row_spec(COL_V // W_C),
                   pl.BlockSpec((ATT_TK, ATT_TK), lambda b, j, pt, s: (0, 0))]
                  + [page_spec(p) for p in range(npg)]
                  + [page_spec(p) for p in range(npg)]
                  + [pl.BlockSpec(memory_space=pl.ANY)]),
        out_specs=pl.BlockSpec((DEC_SEQ, W_C), lambda b, j, pt, s: (rb0 + b, 0)),
        scratch_shapes=[
            pltpu.VMEM((nrow, W_C), BF16),
            pltpu.VMEM((nrow, LANES), F32),
            pltpu.VMEM((PAGE_SIZE, W_C), BF16),
            pltpu.VMEM((PAGE_SIZE, W_C), BF16),
            pltpu.VMEM((W_C, npg * PAGE_SIZE), BF16),
            pltpu.VMEM((W_C, npg * PAGE_SIZE), BF16),
            pltpu.VMEM((nrow, W_C), F32),
            pltpu.VMEM((nrow, LANES), F32),
        ],
    )
    n_in = 2 + 4 + 2 * npg
    return pl.pallas_call(
        _attn_sample_kernel,
        grid_spec=grid_spec,
        out_shape=jax.ShapeDtypeStruct((N_TOK, W_C), F32),
        input_output_aliases={n_in: 0},
        compiler_params=_cparams(("parallel", "arbitrary")),
        name="attn_sample",
    )(page_table, sb_bias, proj, proj, proj, u, *([cache_k4] * npg), *([cache_v4] * npg), y_c)


def _merge_kernel(yab_ref, yc_ref, ga_ref, gb_ref, gc_ref, x_ref, wa_ref, wb_ref, wc_ref, wo_ref,
                  gn_ref, xo_ref, h2_ref):
    ya = yab_ref[:, 0:W_A].astype(BF16)
    yb = yab_ref[:, W_A:W_A + W_B].astype(BF16)
    yc = yc_ref[...].astype(BF16)
    gate = lambda ref: jax.nn.sigmoid(ref[...].astype(F32))
    m = gate(ga_ref) * jnp.dot(ya, wa_ref[...], preferred_element_type=F32)
    m = m + gate(gb_ref) * jnp.dot(yb, wb_ref[...], preferred_element_type=F32)
    m = m + gate(gc_ref) * jnp.dot(yc, wc_ref[...], preferred_element_type=F32)
    xn = x_ref[...] + jnp.dot(m.astype(BF16), wo_ref[...], preferred_element_type=F32)
    xo_ref[...] = xn
    h2_ref[...] = _rmsnorm(xn, gn_ref[...]).astype(BF16)


def _merge(y_ab, y_c, gates, x, lw):
    tm = 640
    rows = lambda w, c: pl.BlockSpec((tm, w), lambda i: (i, c))
    full = lambda shape: pl.BlockSpec(shape, lambda i: (0, 0))
    return pl.pallas_call(
        _merge_kernel,
        grid=(N_TOK // tm,),
        in_specs=[
            rows(W_A + W_B, 0), rows(W_C, 0),
            rows(D_MODEL, 0), rows(D_MODEL, 1), rows(D_MODEL, 2),
            rows(D_MODEL, 0),
            full((W_A, D_MODEL)), full((W_B, D_MODEL)), full((W_C, D_MODEL)),
            full((D_MODEL, D_MODEL)), full((1, D_MODEL)),
        ],
        out_specs=[rows(D_MODEL, 0), rows(D_MODEL, 0)],
        out_shape=[jax.ShapeDtypeStruct((N_TOK, D_MODEL), F32),
                   jax.ShapeDtypeStruct((N_TOK, D_MODEL), BF16)],
        compiler_params=_cparams(("parallel",)),
        name="merge",
    )(y_ab, y_c, gates, gates, gates, x, lw["w_br_a"], lw["w_br_b"], lw["w_br_c"], lw["w_out"],
      lw["norm_ffn"])


def _router_kernel(x_ref, gn_ref, wr_ref, u_ref, route_ref, cnt_ref, cnt_scr):
    @pl.when(pl.program_id(0) == 0)
    def _():
        cnt_scr[...] = jnp.zeros_like(cnt_scr)

    h = _rmsnorm(x_ref[...], gn_ref[...])
    logits = jnp.dot(h, wr_ref[...], preferred_element_type=F32, precision=lax.Precision.HIGHEST)
    lane = lax.broadcasted_iota(jnp.int32, logits.shape, 1)
    neg = jnp.float32(-jnp.inf)
    logits = jnp.where(lane < N_EXPERTS, logits, neg)
    v1 = jnp.max(logits, axis=-1, keepdims=True)
    i1 = jnp.min(jnp.where(logits == v1, lane, LANES), axis=-1, keepdims=True)
    rest = jnp.where(lane == i1, neg, logits)
    v2 = jnp.max(rest, axis=-1, keepdims=True)
    i2 = jnp.min(jnp.where(rest == v2, lane, LANES), axis=-1, keepdims=True)
    e2 = jnp.exp(v2 - v1)
    den = 1.0 + e2
    oh1 = lane == i1
    oh2 = lane == i2
    m = jnp.where(oh1 | oh2, 1.0, 0.0)
    run = cnt_scr[0:1, :]
    befores = []
    for sb in range(ROUTER_TM // MOE_TM):
        m_sb = m[sb * MOE_TM:(sb + 1) * MOE_TM]
        befores.append(jnp.dot(u_ref[...], m_sb.astype(BF16), preferred_element_type=F32) + run)
        run = run + jnp.sum(m_sb, axis=0, keepdims=True)
    before = jnp.concatenate(befores, axis=0)
    r1 = jnp.sum(jnp.where(oh1, before, 0.0), axis=-1, keepdims=True)
    r2 = jnp.sum(jnp.where(oh2, before, 0.0), axis=-1, keepdims=True)
    cols = (1.0 / den, e2 / den, i1.astype(F32), i2.astype(F32), r1, r2)
    route = jnp.zeros(logits.shape, F32)
    for k, c in enumerate(cols):
        route = jnp.where(lane == k, c, route)
    route_ref[...] = route
    cnt = jnp.broadcast_to(run, cnt_scr.shape)
    cnt_scr[...] = cnt
    cnt_ref[...] = cnt


def _router(x, g, router_pad, u):
    tm = ROUTER_TM
    return pl.pallas_call(
        _router_kernel,
        grid=(N_TOK // tm,),
        in_specs=[
            pl.BlockSpec((tm, D_MODEL), lambda i: (i, 0)),
            pl.BlockSpec((1, D_MODEL), lambda i: (0, 0)),
            pl.BlockSpec((D_MODEL, LANES), lambda i: (0, 0)),
            pl.BlockSpec((MOE_TM, MOE_TM), lambda i: (0, 0)),
        ],
        out_specs=[pl.BlockSpec((tm, LANES), lambda i: (i, 0)),
                   pl.BlockSpec((SUBLANES, LANES), lambda i: (0, 0))],
        out_shape=[jax.ShapeDtypeStruct((N_TOK, LANES), F32),
                   jax.ShapeDtypeStruct((SUBLANES, LANES), F32)],
        scratch_shapes=[pltpu.VMEM((SUBLANES, LANES), F32)],
        compiler_params=_cparams(("arbitrary",)),
        name="router",
    )(x, g.reshape(1, D_MODEL), router_pad, u)


def _route_plan(route, cnt):
    t = MOE_T
    e12 = route[:, 2:4].astype(jnp.int32)
    rank = route[:, 4:6].astype(jnp.int32)
    counts = cnt[0, :N_EXPERTS].astype(jnp.int32)
    padded = ((counts + t - 1) // t) * t
    ends = jnp.cumsum(padded)
    pos = (ends - padded)[e12] + rank
    n_active = ends[-1] // t
    tile_start = jnp.arange(MOE_ROWS // t, dtype=jnp.int32) * t
    tile_expert = jnp.sum(tile_start[:, None] >= ends[None, :], axis=1).astype(jnp.int32)
    last_expert = jnp.sum((n_active - 1) * t >= ends).astype(jnp.int32)
    tile_expert = jnp.minimum(tile_expert, last_expert)
    pos_blk = pos.reshape(N_TOK // MOE_TM, MOE_TM, 2).transpose(0, 2, 1)
    return pos_blk, tile_expert, n_active.reshape(1).astype(jnp.int32)


def _row(ref, r):
    return ref.at[pl.ds(r, 1), :]


def _dispatch_kernel(pos_ref, x_ref, xs_in_hbm, xs_hbm, sem):
    del xs_in_hbm

    def issue(r, c):
        pltpu.make_async_copy(_row(x_ref, r), _row(xs_hbm, pos_ref[0, r]), sem).start()
        pltpu.make_async_copy(_row(x_ref, r), _row(xs_hbm, pos_ref[1, r]), sem).start()
        return c

    def drain(r, c):
        pltpu.make_async_copy(_row(x_ref, 0), _row(xs_hbm, 0), sem).wait()
        pltpu.make_async_copy(_row(x_ref, 0), _row(xs_hbm, 0), sem).wait()
        return c

    lax.fori_loop(0, MOE_TM, issue, 0, unroll=8)
    lax.fori_loop(0, MOE_TM, drain, 0, unroll=8)


def _dispatch(pos_blk, x):
    zeros = jnp.zeros((MOE_ROWS, D_MODEL), F32)
    return pl.pallas_call(
        _dispatch_kernel,
        grid=(N_TOK // MOE_TM,),
        in_specs=[
            pl.BlockSpec((None, 2, MOE_TM), lambda i: (i, 0, 0), memory_space=pltpu.SMEM),
            pl.BlockSpec((MOE_TM, D_MODEL), lambda i: (i, 0)),
            pl.BlockSpec(memory_space=pl.ANY),
        ],
        out_specs=pl.BlockSpec(memory_space=pl.ANY),
        out_shape=jax.ShapeDtypeStruct((MOE_ROWS, D_MODEL), F32),
        scratch_shapes=[pltpu.SemaphoreType.DMA(())],
        input_output_aliases={2: 0},
        compiler_params=_cparams(("arbitrary",)),
        name="moe_dispatch",
    )(pos_blk, x, zeros)


def _moe_ffn_kernel(te_ref, na_ref, xs_ref, gn_ref, w1_ref, w3_ref, w2_ref, ys_ref, hb_scr, acc_scr):
    del te_ref
    f = pl.program_id(1)
    active = pl.program_id(0) < na_ref[0]

    @pl.when(active & (f == 0))
    def _():
        hb_scr[...] = _rmsnorm(xs_ref[...], gn_ref[...]).astype(BF16)
        acc_scr[...] = jnp.zeros_like(acc_scr)

    @pl.when(active)
    def _():
        h = hb_scr[...]
        a = jnp.dot(h, w1_ref[...], preferred_element_type=F32)
        b = jnp.dot(h, w3_ref[...], preferred_element_type=F32)
        t = (a * jax.nn.sigmoid(a)) * b
        acc_scr[...] += jnp.dot(t.astype(BF16), w2_ref[...], preferred_element_type=F32)

    @pl.when(f == pl.num_programs(1) - 1)
    def _():
        ys_ref[...] = jnp.where(active, acc_scr[...], 0.0)


def _moe_ffn(tile_expert, n_active, xs, g, w1, w3, w2):
    t, tf = MOE_T, MOE_TF
    nf = w1.shape[2] // tf

    def wspec(shape, order):
        def index(r, f, te, na):
            ff = jnp.where(r < na[0], f, nf - 1)
            return (te[r], ff, 0) if order else (te[r], 0, ff)
        return pl.BlockSpec(shape, index)

    grid_spec = pltpu.PrefetchScalarGridSpec(
        num_scalar_prefetch=2,
        grid=(MOE_ROWS // t, nf),
        in_specs=[
            pl.BlockSpec((t, D_MODEL), lambda r, f, te, na: (r, 0)),
            pl.BlockSpec((1, D_MODEL), lambda r, f, te, na: (0, 0)),
            wspec((None, D_MODEL, tf), False),
            wspec((None, D_MODEL, tf), False),
            wspec((None, tf, D_MODEL), True),
        ],
        out_specs=pl.BlockSpec((t, D_MODEL), lambda r, f, te, na: (r, 0)),
        scratch_shapes=[pltpu.VMEM((t, D_MODEL), BF16), pltpu.VMEM((t, D_MODEL), F32)],
    )
    return pl.pallas_call(
        _moe_ffn_kernel,
        grid_spec=grid_spec,
        out_shape=jax.ShapeDtypeStruct((MOE_ROWS, D_MODEL), F32),
        compiler_params=_cparams(("parallel", "arbitrary")),
        name="moe_ffn",
    )(tile_expert, n_active, xs, g.reshape(1, D_MODEL), w1, w3, w2)


def _combine_kernel(pos_ref, route_ref, x_ref, gf_ref, ys_hbm, *rest, final_norm):
    out_refs, (y1_buf, y2_buf, sem) = rest[:-3], rest[-3:]
    def issue(r, c):
        pltpu.make_async_copy(_row(ys_hbm, pos_ref[0, r]), _row(y1_buf, r), sem).start()
        pltpu.make_async_copy(_row(ys_hbm, pos_ref[1, r]), _row(y2_buf, r), sem).start()
        return c

    def drain(r, c):
        pltpu.make_async_copy(_row(ys_hbm, 0), _row(y1_buf, 0), sem).wait()
        pltpu.make_async_copy(_row(ys_hbm, 0), _row(y2_buf, 0), sem).wait()
        return c

    lax.fori_loop(0, MOE_TM, issue, 0, unroll=8)
    lax.fori_loop(0, MOE_TM, drain, 0, unroll=8)
    y = x_ref[...] + (route_ref[:, 0:1] * y1_buf[...] + route_ref[:, 1:2] * y2_buf[...])
    if final_norm:
        y = _rmsnorm(y, gf_ref[...])
        is_prompt = pl.program_id(0) < N_PROMPT // MOE_TM

        @pl.when(is_prompt)
        def _():
            out_refs[0][...] = y

        @pl.when(jnp.logical_not(is_prompt))
        def _():
            out_refs[1][...] = y
    else:
        out_refs[0][...] = y


def _combine(pos_blk, route, x, g_final, ys, *, final_norm):
    tm = MOE_TM
    n_p = N_PROMPT // tm
    if final_norm:
        out_specs = [pl.BlockSpec((tm, D_MODEL), lambda i: (jnp.minimum(i, n_p - 1), 0)),
                     pl.BlockSpec((tm, D_MODEL), lambda i: (jnp.maximum(i - n_p, 0), 0))]
        out_shape = [jax.ShapeDtypeStruct((N_PROMPT, D_MODEL), F32),
                     jax.ShapeDtypeStruct((N_SAMPLE, D_MODEL), F32)]
    else:
        out_specs = pl.BlockSpec((tm, D_MODEL), lambda i: (i, 0))
        out_shape = jax.ShapeDtypeStruct((N_TOK, D_MODEL), F32)
    return pl.pallas_call(
        functools.partial(_combine_kernel, final_norm=final_norm),
        grid=(N_TOK // tm,),
        in_specs=[
            pl.BlockSpec((None, 2, tm), lambda i: (i, 0, 0), memory_space=pltpu.SMEM),
            pl.BlockSpec((tm, LANES), lambda i: (i, 0)),
            pl.BlockSpec((tm, D_MODEL), lambda i: (i, 0)),
            pl.BlockSpec((1, D_MODEL), lambda i: (0, 0)),
            pl.BlockSpec(memory_space=pl.ANY),
        ],
        out_specs=out_specs,
        out_shape=out_shape,
        scratch_shapes=[pltpu.VMEM((tm, D_MODEL), F32), pltpu.VMEM((tm, D_MODEL), F32),
                        pltpu.SemaphoreType.DMA(())],
        compiler_params=_cparams(("arbitrary",)),
        name="moe_combine",
    )(pos_blk, route, x, g_final.reshape(1, D_MODEL), ys)


def _ffn_kernel(h_ref, x_ref, w1_ref, w3_ref, w2_ref, gf_ref, o_ref, acc_scr, *, final_norm):
    f = pl.program_id(1)

    @pl.when(f == 0)
    def _():
        acc_scr[...] = jnp.zeros_like(acc_scr)

    h = h_ref[...]
    a = jnp.dot(h, w1_ref[...], preferred_element_type=F32)
    b = jnp.dot(h, w3_ref[...], preferred_element_type=F32)
    t = (a * jax.nn.sigmoid(a)) * b
    acc_scr[...] += jnp.dot(t.astype(BF16), w2_ref[...], preferred_element_type=F32)

    @pl.when(f == pl.num_programs(1) - 1)
    def _():
        y = x_ref[...] + acc_scr[...]
        if final_norm:
            y = _rmsnorm(y, gf_ref[...])
        o_ref[...] = y


def _ffn(h2, x, w1, w3, w2, g_final, *, final_norm):
    tm, tf = 1280, 256
    d_ff = w1.shape[1]
    return pl.pallas_call(
        functools.partial(_ffn_kernel, final_norm=final_norm),
        grid=(N_TOK // tm, d_ff // tf),
        in_specs=[
            pl.BlockSpec((tm, D_MODEL), lambda i, f: (i, 0)),
            pl.BlockSpec((tm, D_MODEL), lambda i, f: (i, 0)),
            pl.BlockSpec((D_MODEL, tf), lambda i, f: (0, f)),
            pl.BlockSpec((D_MODEL, tf), lambda i, f: (0, f)),
            pl.BlockSpec((tf, D_MODEL), lambda i, f: (f, 0)),
            pl.BlockSpec((1, D_MODEL), lambda i, f: (0, 0)),
        ],
        out_specs=pl.BlockSpec((tm, D_MODEL), lambda i, f: (i, 0)),
        out_shape=jax.ShapeDtypeStruct((N_TOK, D_MODEL), F32),
        scratch_shapes=[pltpu.VMEM((tm, D_MODEL), F32)],
        compiler_params=_cparams(("parallel", "arbitrary")),
        name="ffn_dense",
    )(h2, x, w1, w3, w2, g_final.reshape(1, D_MODEL))


def _pad_hist(hist, width):
    nseq, _, c = hist.shape
    return jnp.concatenate([jnp.zeros((nseq, SUBLANES - (width - 1), c), hist.dtype), hist], axis=1)


def _block_diag(w):
    h, d, _ = w.shape
    eye = jnp.eye(h, dtype=w.dtype)
    return (eye[:, None, :, None] * w[:, :, None, :]).reshape(h * d, h * d)


def kernel(x_prompt, x_sample, cache_k, cache_v, state_conv_a, state_conv_b, state_h, page_table,
           norm_mix, w_in, conv_a_w, conv_b_w, conv_b_bias, rg_w_r, rg_b_r, rg_w_i, rg_b_i,
           rg_lambda, sb_bias, w_br_a, w_br_b, w_br_c, w_out, norm_ffn, ffn_w1, ffn_w3, ffn_w2,
           moe_router, moe_w1, moe_w3, moe_w2, norm_final):
    depth = w_in.shape[0]
    n_phys = cache_k.shape[1]
    cache_k4 = cache_k.transpose(0, 1, 3, 4, 2).reshape(depth, n_phys, W_C, PAGE_SIZE)
    cache_v4 = cache_v.transpose(0, 1, 3, 4, 2).reshape(depth, n_phys, W_C, PAGE_SIZE)
    x = jnp.concatenate([x_prompt.reshape(N_PROMPT, D_MODEL), x_sample.reshape(N_SAMPLE, D_MODEL)])
    ri = lax.broadcasted_iota(jnp.int32, (ATT_TK, ATT_TK), 0)
    ci = lax.broadcasted_iota(jnp.int32, (ATT_TK, ATT_TK), 1)
    u = (ri > ci).astype(BF16)
    zeros8 = jnp.zeros((BATCH, SUBLANES, W_A), F32)

    st_p, st_s = [], []
    kv_prev = None
    for l in range(depth):
        lw = {
            "conv_a_w": conv_a_w[l], "conv_b_w": conv_b_w[l], "conv_b_bias": conv_b_bias[l][None],
            "rg_w_r": _block_diag(rg_w_r[l]).astype(BF16), "rg_b_r": rg_b_r[l].reshape(1, W_B),
            "rg_w_i": _block_diag(rg_w_i[l]).astype(BF16), "rg_b_i": rg_b_i[l].reshape(1, W_B),
            "rg_lambda": rg_lambda[l][None],
            "w_br_a": w_br_a[l].astype(BF16), "w_br_b": w_br_b[l].astype(BF16),
            "w_br_c": w_br_c[l].astype(BF16), "w_out": w_out[l].astype(BF16),
            "norm_ffn": norm_ffn[l][None],
        }
        proj = _in_proj(x, norm_mix[l], w_in[l][:, :COL_G].astype(BF16), COL_G // 2, F32)
        gates = _in_proj(x, norm_mix[l], w_in[l][:, COL_G:].astype(BF16), IN_WIDTH - COL_G, BF16)

        y_ab, sa_p, sb_p, hl_p = _mixers(proj, zeros8, zeros8, zeros8, lw, jnp.zeros((N_TOK, W_C), F32),
                                         nseq=BATCH, t=SEQ, row0=0, first_pos_zero=True)
        h0_8 = jnp.broadcast_to(state_h[l][:, None, :], (DEC_BATCH, SUBLANES, W_B))
        y_ab, sa_s, sb_s, hl_s = _mixers(proj, _pad_hist(state_conv_a[l], CONV_A),
                                         _pad_hist(state_conv_b[l], CONV_B), h0_8, lw, y_ab,
                                         nseq=DEC_BATCH, t=DEC_SEQ, row0=N_PROMPT,
                                         first_pos_zero=False)

        y_c, kt, vt = _attn_prompt(proj, sb_bias[l], u, jnp.zeros((N_TOK, W_C), F32), kv_prev)
        kv_prev = (kt, vt)
        y_c = _attn_sample(proj, y_c, cache_k4, cache_v4, page_table, l, sb_bias[l], u)

        x, h2 = _merge(y_ab, y_c, gates, x, lw)

        last = l == depth - 1
        if l % 2 == 0:
            i = l // 2
            x = _ffn(h2, x, ffn_w1[i].astype(BF16), ffn_w3[i].astype(BF16), ffn_w2[i].astype(BF16),
                     norm_final, final_norm=last)
        else:
            i = l // 2
            router_pad = jnp.pad(moe_router[i], ((0, 0), (0, LANES - N_EXPERTS)))
            route, cnt = _router(x, norm_ffn[l], router_pad, u)
            pos_blk, tile_expert, n_active = _route_plan(route, cnt)
            xs = _dispatch(pos_blk, x)
            ys = _moe_ffn(tile_expert, n_active, xs, norm_ffn[l], moe_w1[i].astype(BF16),
                          moe_w3[i].astype(BF16), moe_w2[i].astype(BF16))
            x = _combine(pos_blk, route, x, norm_final, ys, final_norm=last)
            if last:
                y_prompt, y_sample = x

        ks = proj[N_PROMPT:, COL_K:COL_K + W_C].reshape(DEC_BATCH, DEC_SEQ, H_C, DH_C)
        vs = proj[N_PROMPT:, COL_V:COL_V + W_C].reshape(DEC_BATCH, DEC_SEQ, H_C, DH_C)
        st_p.append((None, None, sa_p[:, SUBLANES - (CONV_A - 1):], sb_p[:, SUBLANES - (CONV_B - 1):],
                     hl_p[:, 0]))
        st_s.append((ks, vs, sa_s[:, SUBLANES - (CONV_A - 1):], sb_s[:, SUBLANES - (CONV_B - 1):],
                     hl_s[:, 0]))

    if depth % 2 == 1:
        y_prompt, y_sample = x[:N_PROMPT], x[N_PROMPT:]
    y_prompt = y_prompt.reshape(BATCH, SEQ, D_MODEL)
    y_sample = y_sample.reshape(DEC_BATCH, DEC_SEQ, D_MODEL)
    stack = lambda sts, k: jnp.stack([s[k] for s in sts])
    k_prompt, v_prompt = (a.reshape(depth, BATCH, H_C, DH_C, SEQ).transpose(0, 1, 4, 2, 3) for a in kv_prev)
    return (y_prompt, y_sample,
            k_prompt, v_prompt, stack(st_p, 2), stack(st_p, 3), stack(st_p, 4),
            stack(st_s, 0), stack(st_s, 1), stack(st_s, 2), stack(st_s, 3), stack(st_s, 4))
```
